```python
import jax, jax.numpy as jnp
from jax import lax
import numpy as np

D_MODEL = 1024
BATCH = 1
SEQ = 16384
DEPTH = 1
DEC_BATCH = 32
DEC_SEQ = 4
PAST_LEN = 16384
PAGE_SIZE = 128

N_HEADS = 8
HEAD_DIM = 64
ATTN_DIM = N_HEADS * HEAD_DIM
IDX_HEADS = 8
IDX_DIM = 64
TOPK_MAX = 256
Q_BLOCK = 128
CONV_DIM = 512
CONV_WIDTH = 31
PEER_HEADS = 8
N_KEYS = 128
N_EXPERTS = N_KEYS * N_KEYS
PEER_KEY_DIM = 256
PEER_HALF = PEER_KEY_DIM // 2
PEER_TOPK = 16
PEER_BLOCK = 128
EPS = 1e-6

IN_SPLITS = (ATTN_DIM, ATTN_DIM, ATTN_DIM, IDX_HEADS * IDX_DIM, IDX_DIM, IDX_HEADS, 2 * CONV_DIM, D_MODEL, D_MODEL)
IN_DIM = sum(IN_SPLITS)
IN_OFFSETS = tuple(int(o) for o in np.cumsum(IN_SPLITS)[:-1])

kernel_name = "dsa_conformer_peer_hybrid_step"


def rms_norm(x, g):
    xf = x.astype(jnp.float32)
    y = xf * lax.rsqrt(jnp.mean(xf * xf, axis=-1, keepdims=True) + EPS)
    return (y * g.astype(jnp.float32)).astype(x.dtype)


def layer_norm(x, g, b):
    xf = x.astype(jnp.float32)
    mu = jnp.mean(xf, axis=-1, keepdims=True)
    var = jnp.mean(jnp.square(xf - mu), axis=-1, keepdims=True)
    y = (xf - mu) * lax.rsqrt(var + EPS)
    return (y * g.astype(jnp.float32) + b.astype(jnp.float32)).astype(x.dtype)


def mixer_inputs(x, ln1_g, w_in, q_norm_g, k_norm_g):
    B, T, _ = x.shape
    h = rms_norm(x, ln1_g)
    z = h @ w_in
    q, k, v, qi, ki, wi, conv_in, ga, gc = jnp.split(z, IN_OFFSETS, axis=-1)
    q = rms_norm(q.reshape(B, T, N_HEADS, HEAD_DIM), q_norm_g)
    k = rms_norm(k.reshape(B, T, N_HEADS, HEAD_DIM), k_norm_g)
    v = v.reshape(B, T, N_HEADS, HEAD_DIM)
    qi = qi.reshape(B, T, IDX_HEADS, IDX_DIM)
    a, b = jnp.split(conv_in, 2, axis=-1)
    u = a * jax.nn.sigmoid(b)
    return q, k, v, qi, ki, wi, u, ga, gc


def indexer_topk(qi, wi, ki, q_pos, topk):
    s = jnp.einsum('bthd,bsd->bths', qi.astype(jnp.float32), ki.astype(jnp.float32))
    score = jnp.einsum('bth,bths->bts', wi.astype(jnp.float32), jax.nn.relu(s))
    key_pos = jnp.arange(ki.shape[1])
    allowed = key_pos[None, :] <= q_pos[:, None]
    score = jnp.where(allowed[None], score, -jnp.inf)
    _, sel = lax.top_k(score, topk)
    valid = sel <= q_pos[None, :, None]
    return sel, valid


def sparse_attend(q, k_sel, v_sel, valid):
    logits = jnp.einsum('bthd,btkhd->bthk', q.astype(jnp.float32), k_sel.astype(jnp.float32)) * (HEAD_DIM ** -0.5)
    logits = jnp.where(valid[:, :, None, :], logits, -jnp.inf)
    p = jax.nn.softmax(logits, axis=-1)
    return jnp.einsum('bthk,btkhd->bthd', p, v_sel.astype(jnp.float32)).astype(q.dtype)


def conv_module(u_pad, conv_dw_w, conv_dw_b, conv_ln_g, conv_ln_b, w_conv_out):
    y = lax.conv_general_dilated(u_pad, conv_dw_w[:, None, :], window_strides=(1,), padding='VALID',
                                 dimension_numbers=('NWC', 'WIO', 'NWC'),
                                 feature_group_count=CONV_DIM) + conv_dw_b
    y = jax.nn.silu(layer_norm(y, conv_ln_g, conv_ln_b))
    return y @ w_conv_out


def merge_branches(x, attn, conv_y, ga, gc, w_o_attn, w_out):
    B, T, _ = x.shape
    a = attn.reshape(B, T, ATTN_DIM) @ w_o_attn
    m = jax.nn.sigmoid(ga) * a + jax.nn.sigmoid(gc) * conv_y
    return x + m @ w_out


def peer_block(h, peer_w_q, peer_sub_keys, peer_u, peer_v):
    n = h.shape[0]
    q = jnp.einsum('nd,dhk->nhk', h, peer_w_q).reshape(n, PEER_HEADS, 2, PEER_HALF)
    s = jnp.einsum('nhcd,hckd->nhck', q.astype(jnp.float32), peer_sub_keys.astype(jnp.float32))
    v_half, i_half = lax.top_k(s, PEER_TOPK)
    cand = (v_half[:, :, 0, :, None] + v_half[:, :, 1, None, :]).reshape(n, PEER_HEADS, PEER_TOPK * PEER_TOPK)
    top_v, top_i = lax.top_k(cand, PEER_TOPK)
    i1 = jnp.take_along_axis(i_half[:, :, 0, :], top_i // PEER_TOPK, axis=-1)
    i2 = jnp.take_along_axis(i_half[:, :, 1, :], top_i % PEER_TOPK, axis=-1)
    eid = i1 * N_KEYS + i2
    g = jax.nn.softmax(top_v, axis=-1)
    act = jax.nn.gelu(jnp.einsum('nd,nhkd->nhk', h, peer_u[eid]))
    return jnp.einsum('nhk,nhkd->nd', (g * act).astype(h.dtype), peer_v[eid])


def peer_ffn(x, ln2_g, peer_w_q, peer_sub_keys, peer_u, peer_v):
    B, T, D = x.shape
    n = B * T
    n_blocks = -(-n // PEER_BLOCK)
    h = rms_norm(x, ln2_g).reshape(n, D)
    h = jnp.pad(h, ((0, n_blocks * PEER_BLOCK - n), (0, 0)))
    out = lax.map(lambda hb: peer_block(hb, peer_w_q, peer_sub_keys, peer_u, peer_v),
                  h.reshape(n_blocks, PEER_BLOCK, D))
    return x + out.reshape(n_blocks * PEER_BLOCK, D)[:n].reshape(B, T, D)


def setup_inputs(seed: int = 0) -> dict:
    key = jax.random.key(seed)
    ks = jax.random.split(key, 24)
    n_pages = PAST_LEN // PAGE_SIZE
    used = DEC_BATCH * n_pages
    n_pool = used + max(1, used // 4)

    def nrm(k, shape, scale=1.0):
        return jax.random.normal(k, shape, jnp.float32) * scale

    page_table = jax.random.permutation(ks[0], n_pool)[:used].reshape(DEC_BATCH, n_pages).astype(jnp.int32)
    return {
        "x_prompt": nrm(ks[1], (BATCH, SEQ, D_MODEL)),
        "x_sample": nrm(ks[2], (DEC_BATCH, DEC_SEQ, D_MODEL)),
        "cache_k": nrm(ks[3], (n_pool, PAGE_SIZE, N_HEADS, HEAD_DIM)),
        "cache_v": nrm(ks[4], (n_pool, PAGE_SIZE, N_HEADS, HEAD_DIM)),
        "cache_idx_k": nrm(ks[5], (n_pool, PAGE_SIZE, IDX_DIM)),
        "state_conv": nrm(ks[6], (DEC_BATCH, CONV_WIDTH - 1, CONV_DIM), 0.5),
        "page_table": page_table,
        "ln1_g": 1.0 + nrm(ks[7], (D_MODEL,), 0.02),
        "w_in": nrm(ks[8], (D_MODEL, IN_DIM), D_MODEL ** -0.5),
        "q_norm_g": 1.0 + nrm(ks[9], (HEAD_DIM,), 0.02),
        "k_norm_g": 1.0 + nrm(ks[10], (HEAD_DIM,), 0.02),
        "w_o_attn": nrm(ks[11], (ATTN_DIM, D_MODEL), ATTN_DIM ** -0.5),
        "conv_dw_w": nrm(ks[12], (CONV_WIDTH, CONV_DIM), CONV_WIDTH ** -0.5),
        "conv_dw_b": nrm(ks[13], (CONV_DIM,), 0.02),
        "conv_ln_g": 1.0 + nrm(ks[14], (CONV_DIM,), 0.02),
        "conv_ln_b": nrm(ks[15], (CONV_DIM,), 0.02),
        "w_conv_out": nrm(ks[16], (CONV_DIM, D_MODEL), CONV_DIM ** -0.5),
        "w_out": nrm(ks[17], (D_MODEL, D_MODEL), D_MODEL ** -0.5),
        "ln2_g": 1.0 + nrm(ks[18], (D_MODEL,), 0.02),
        "peer_w_q": nrm(ks[19], (D_MODEL, PEER_HEADS, PEER_KEY_DIM), D_MODEL ** -0.5),
        "peer_sub_keys": nrm(ks[20], (PEER_HEADS, 2, N_KEYS, PEER_HALF), PEER_HALF ** -0.5),
        "peer_u": nrm(ks[21], (N_EXPERTS, D_MODEL), D_MODEL ** -0.5),
        "peer_v": nrm(ks[22], (N_EXPERTS, D_MODEL), PEER_HEADS ** -0.5),
    }


def reference(x_prompt, x_sample, cache_k, cache_v, cache_idx_k, state_conv, page_table,
              ln1_g, w_in, q_norm_g, k_norm_g, w_o_attn, conv_dw_w, conv_dw_b, conv_ln_g, conv_ln_b,
              w_conv_out, w_out, ln2_g, peer_w_q, peer_sub_keys, peer_u, peer_v):
    gather_rows = jax.vmap(lambda rows, idx: rows[idx])

    B, T, _ = x_prompt.shape
    q, k_p, v_p, qi, ki_p, wi, u, ga, gc = mixer_inputs(x_prompt, ln1_g, w_in, q_norm_g, k_norm_g)
    topk_p = min(TOPK_MAX, T // 4)

    def attend_block(i):
        start = i * Q_BLOCK
        q_b = lax.dynamic_slice_in_dim(q, start, Q_BLOCK, axis=1)
        qi_b = lax.dynamic_slice_in_dim(qi, start, Q_BLOCK, axis=1)
        wi_b = lax.dynamic_slice_in_dim(wi, start, Q_BLOCK, axis=1)
        q_pos = start + jnp.arange(Q_BLOCK)
        sel, valid = indexer_topk(qi_b, wi_b, ki_p, q_pos, topk_p)
        return sparse_attend(q_b, gather_rows(k_p, sel), gather_rows(v_p, sel), valid)

    attn_p = lax.map(attend_block, jnp.arange(T // Q_BLOCK))
    attn_p = jnp.moveaxis(attn_p, 0, 1).reshape(B, T, N_HEADS, HEAD_DIM)
    u_pad = jnp.concatenate([jnp.zeros((B, CONV_WIDTH - 1, CONV_DIM), u.dtype), u], axis=1)
    conv_p_state = u_pad[:, -(CONV_WIDTH - 1):]
    conv_y = conv_module(u_pad, conv_dw_w, conv_dw_b, conv_ln_g, conv_ln_b, w_conv_out)
    x1 = merge_branches(x_prompt, attn_p, conv_y, ga, gc, w_o_attn, w_out)
    y_prompt = peer_ffn(x1, ln2_g, peer_w_q, peer_sub_keys, peer_u, peer_v)

    DB, DS, _ = x_sample.shape
    n_pages = PAST_LEN // PAGE_SIZE
    q_s, k_s, v_s, qi_s, ki_s, wi_s, u_s, ga_s, gc_s = mixer_inputs(x_sample, ln1_g, w_in, q_norm_g, k_norm_g)
    ki_past = cache_idx_k[page_table].reshape(DB, n_pages * PAGE_SIZE, IDX_DIM)
    ki_all = jnp.concatenate([ki_past, ki_s.astype(ki_past.dtype)], axis=1)
    q_pos_s = PAST_LEN + jnp.arange(DS)
    topk_s = min(TOPK_MAX, (PAST_LEN + DS) // 4)
    sel_s, valid_s = indexer_topk(qi_s, wi_s, ki_all, q_pos_s, topk_s)
    is_past = sel_s < PAST_LEN
    pidx = jnp.minimum(sel_s, PAST_LEN - 1)
    phys = gather_rows(page_table, pidx // PAGE_SIZE)
    off = pidx % PAGE_SIZE
    nidx = jnp.clip(sel_s - PAST_LEN, 0, DS - 1)
    k_sel = jnp.where(is_past[..., None, None], cache_k[phys, off], gather_rows(k_s, nidx))
    v_sel = jnp.where(is_past[..., None, None], cache_v[phys, off], gather_rows(v_s, nidx))
    attn_s = sparse_attend(q_s, k_sel, v_sel, valid_s)
    u_pad_s = jnp.concatenate([state_conv.astype(u_s.dtype), u_s], axis=1)
    conv_s_state = u_pad_s[:, -(CONV_WIDTH - 1):]
    conv_y_s = conv_module(u_pad_s, conv_dw_w, conv_dw_b, conv_ln_g, conv_ln_b, w_conv_out)
    x1_s = merge_branches(x_sample, attn_s, conv_y_s, ga_s, gc_s, w_o_attn, w_out)
    y_sample = peer_ffn(x1_s, ln2_g, peer_w_q, peer_sub_keys, peer_u, peer_v)

    return (y_prompt, y_sample, k_p, v_p, ki_p, conv_p_state, k_s, v_s, ki_s, conv_s_state)
```

```python
import functools

import jax
import jax.numpy as jnp
import numpy as np
from jax import lax
from jax.experimental import pallas as pl
from jax.experimental.pallas import tpu as pltpu

F32 = jnp.float32
BF16 = jnp.bfloat16
I32 = jnp.int32

EPS = 1e-6
N_HEADS = 8
HEAD_DIM = 64
ATTN_DIM = N_HEADS * HEAD_DIM
IDX_HEADS = 8
IDX_DIM = 64
TOPK_MAX = 256
CONV_DIM = 512
PEER_TOPK = 16
LANES = 128
SELECT_ROWS = 128
ROW_TILE = 256
Q_TILE = 128
PEER_TOKEN_TILE = 512
PEER_EXPERT_TILE = 512
VMEM_LIMIT = 56 * 1024 * 1024

NEG_INF = float("-inf")
INT_MIN = -(2 ** 31)


def _cparams(*sem):
    return pltpu.CompilerParams(dimension_semantics=sem, vmem_limit_bytes=VMEM_LIMIT)


def _const_spec(shape):
    nd = len(shape)
    return pl.BlockSpec(shape, lambda *_: (0,) * nd, pipeline_mode=pl.Buffered(1))


def _group_rms(x, gsum_ref, gain):
    x2 = x * x
    hi = x2.astype(BF16)
    lo = (x2 - hi.astype(F32)).astype(BF16)
    ss = (jnp.dot(hi, gsum_ref[...], preferred_element_type=F32)
          + jnp.dot(lo, gsum_ref[...], preferred_element_type=F32))
    return x * lax.rsqrt(ss * (1.0 / HEAD_DIM) + EPS) * gain


def _pad_heads(x):
    lo_half = lax.broadcasted_iota(I32, (x.shape[0], LANES), 1) < HEAD_DIM
    outs = []
    for p in range(x.shape[1] // LANES):
        slab = x[:, p * LANES:(p + 1) * LANES]
        outs.append(jnp.where(lo_half, slab, 0.0))
        outs.append(jnp.where(lo_half, pltpu.roll(slab, HEAD_DIM, 1), 0.0))
    return jnp.concatenate(outs, axis=1)


def _proj_kernel(x_ref, g1_ref, w_ref, qg_ref, kg_ref, gsum_ref,
                 q_ref, k_ref, kt_ref, v_ref, vb_ref, qi_ref, ki_ref, kit_ref, wi_ref, u_ref, ga_ref, gc_ref):
    x = x_ref[...]
    tm = x.shape[0]
    ms = jnp.mean(x * x, axis=-1, keepdims=True)
    h = (x * lax.rsqrt(ms + EPS) * g1_ref[...]).astype(BF16)

    def seg(lo, width):
        return jnp.dot(h, w_ref[:, lo:lo + width], preferred_element_type=F32)

    A = ATTN_DIM
    q = _group_rms(seg(0, A), gsum_ref, qg_ref[...])
    q_ref[...] = _pad_heads(q * (HEAD_DIM ** -0.5)).astype(BF16)
    k = _group_rms(seg(A, A), gsum_ref, kg_ref[...])
    k_ref[...] = k
    kt_ref[...] = jnp.concatenate([k.T, jnp.zeros((HEAD_DIM, tm), F32)], axis=0).astype(BF16)
    v = seg(2 * A, A)
    v_ref[...] = v
    vb_ref[...] = v.astype(BF16)
    qi_ref[...] = _pad_heads(seg(3 * A, A)).astype(BF16)
    o = 4 * A
    ki = seg(o, LANES)
    ki_ref[...] = ki[:, :IDX_DIM]
    kit_ref[...] = ki.T.astype(BF16)
    wi = seg(o + LANES, LANES)
    wi_ref[...] = jnp.concatenate(
        [jnp.broadcast_to(wi[:, hh:hh + 1], (tm, LANES)) for hh in range(IDX_HEADS)], axis=1)
    o += 2 * LANES
    a = seg(o, CONV_DIM)
    b = seg(o + CONV_DIM, CONV_DIM)
    u_ref[...] = a * jax.nn.sigmoid(b)
    o += 2 * CONV_DIM
    d = x.shape[-1]
    ga_ref[...] = jax.nn.sigmoid(seg(o, d))
    gc_ref[...] = jax.nn.sigmoid(seg(o + d, d))


def _pack_w_in(w_in, d):
    A = ATTN_DIM
    o = 4 * A
    pad = lambda w: jnp.pad(w, ((0, 0), (0, LANES - w.shape[1])))
    parts = [w_in[:, :o], pad(w_in[:, o:o + IDX_DIM]), pad(w_in[:, o + IDX_DIM:o + IDX_DIM + IDX_HEADS]),
             w_in[:, o + IDX_DIM + IDX_HEADS:]]
    return jnp.concatenate(parts, axis=1).astype(BF16)


def _proj(x2d, ln1_g, w_packed, q_norm_g, k_norm_g, tm):
    n, d = x2d.shape
    A = ATTN_DIM
    wcols = w_packed.shape[1]
    gsum = jnp.asarray(np.kron(np.eye(N_HEADS), np.ones((HEAD_DIM, HEAD_DIM))), BF16)
    row = lambda w: pl.BlockSpec((tm, w), lambda i: (i, 0))
    colT = lambda r: pl.BlockSpec((r, tm), lambda i: (0, i))
    sds = jax.ShapeDtypeStruct
    return pl.pallas_call(
        _proj_kernel,
        grid=(n // tm,),
        in_specs=[row(d), _const_spec((1, d)), _const_spec((d, wcols)), _const_spec((1, A)), _const_spec((1, A)),
                  _const_spec((A, A))],
        out_specs=[row(2 * A), row(A), colT(A + HEAD_DIM), row(A), row(A), row(2 * A), row(IDX_DIM), colT(LANES),
                   row(IDX_HEADS * LANES), row(CONV_DIM), row(d), row(d)],
        out_shape=[sds((n, 2 * A), BF16), sds((n, A), F32), sds((A + HEAD_DIM, n), BF16), sds((n, A), F32),
                   sds((n, A), BF16), sds((n, 2 * A), BF16), sds((n, IDX_DIM), F32), sds((LANES, n), BF16),
                   sds((n, IDX_HEADS * LANES), F32), sds((n, CONV_DIM), F32), sds((n, d), F32), sds((n, d), F32)],
        compiler_params=_cparams("parallel"),
        name="proj",
    )(x2d, ln1_g.reshape(1, d), w_packed, jnp.tile(q_norm_g, N_HEADS).reshape(1, A),
      jnp.tile(k_norm_g, N_HEADS).reshape(1, A), gsum)


def _sortable_key(score):
    b = lax.bitcast_convert_type(jnp.where(score == 0.0, 0.0, score), I32)
    return b ^ ((b >> 31) & 0x7FFFFFFF)


def _count_rows(s_ref, r0, rows, n_chunks, pred):
    col0 = lax.broadcasted_iota(I32, (rows, LANES), 1)

    def body(c, acc):
        off = pl.multiple_of(c * LANES, LANES)
        return acc + jnp.where(pred(s_ref[r0:r0 + rows, pl.ds(off, LANES)], col0 + off), 1, 0)

    acc = lax.fori_loop(0, n_chunks, body, jnp.zeros((rows, LANES), I32))
    return jnp.broadcast_to(jnp.sum(acc, axis=1, keepdims=True), (rows, LANES))


def _select_threshold(s_ref, r0, rows, n_chunks, k, n_valid=None):
    count = functools.partial(_count_rows, s_ref, r0, rows, n_chunks)

    def bit_step(it, prefix):
        cand = prefix | jnp.left_shift(jnp.int32(1), 31 - it)
        cnt = count(lambda keys, _: keys >= (cand ^ INT_MIN))
        return jnp.where(cnt >= k, cand, prefix)

    thr = lax.fori_loop(0, 32, bit_step, jnp.zeros((rows, LANES), I32)) ^ INT_MIN
    n_ge = count(lambda keys, _: keys >= thr)
    if n_valid is not None:
        n_ge = jnp.where(lax.broadcasted_iota(I32, (rows, LANES), 0) < n_valid, n_ge, 0)

    @pl.when(jnp.max(n_ge) > k)
    def _demote_surplus_ties():
        need = k - count(lambda keys, _: keys > thr)

        def idx_step(it, prefix):
            cand = prefix | jnp.left_shift(jnp.int32(1), 14 - it)
            cnt = count(lambda keys, idx: jnp.where(keys == thr, idx, cand) < cand)
            return jnp.where(cnt < need, cand, prefix)

        last = lax.fori_loop(0, 15, idx_step, jnp.zeros((rows, LANES), I32))
        col0 = lax.broadcasted_iota(I32, (rows, LANES), 1)

        def demote(c, carry):
            off = pl.multiple_of(c * LANES, LANES)
            keys = s_ref[r0:r0 + rows, pl.ds(off, LANES)]
            surplus = jnp.where(keys == thr, col0 + off, 0) > last
            s_ref[r0:r0 + rows, pl.ds(off, LANES)] = jnp.where(surplus, keys - 1, keys)
            return carry

        lax.fori_loop(0, n_chunks, demote, 0)

    return thr


def _softmax_chunk(h, logits, m_ref, l_ref):
    m_old = m_ref[:, h:h + 1]
    m_new = jnp.maximum(m_old, jnp.max(logits, axis=1, keepdims=True))
    m_safe = jnp.where(m_new == NEG_INF, 0.0, m_new)
    p = jnp.exp(logits - m_safe)
    alpha = jnp.exp(m_old - m_safe)
    l_ref[:, h:h + 1] = l_ref[:, h:h + 1] * alpha + jnp.sum(p, axis=1, keepdims=True)
    m_ref[:, h:h + 1] = m_new
    return p, alpha


def _attn_prompt_kernel(q_ref, qi_ref, wi_ref, kt_ref, vb_ref, kit_ref, o_ref, s_ref, acc_ref, m_ref, l_ref, *, topk):
    i = pl.program_id(0)
    tq = q_ref.shape[0]
    n_blocks = i + 1
    row = lax.broadcasted_iota(I32, (tq, tq), 0)
    col = lax.broadcasted_iota(I32, (tq, tq), 1)
    causal_diag = col <= row
    lo_half = lax.broadcasted_iota(I32, (tq, LANES), 1) < HEAD_DIM

    def score_block(j, carry):
        off = pl.multiple_of(j * tq, tq)
        kit = kit_ref[:, pl.ds(off, tq)]
        sc = jnp.zeros((tq, tq), F32)
        for h in range(IDX_HEADS):
            s = jnp.dot(qi_ref[:, h * LANES:(h + 1) * LANES], kit, preferred_element_type=F32)
            w = wi_ref[:, h * LANES:(h + 1) * LANES]
            sc = sc + jnp.concatenate([w] * (tq // LANES), axis=1) * jnp.maximum(s, 0.0)
        sc = jnp.where(causal_diag | (j < i), sc, NEG_INF)
        s_ref[:, pl.ds(off, tq)] = _sortable_key(sc)
        return carry

    lax.fori_loop(0, n_blocks, score_block, 0)

    n_chunks = n_blocks * (tq // LANES)
    thr = jnp.concatenate([_select_threshold(s_ref, r0, SELECT_ROWS, n_chunks, topk)[:, :1]
                           for r0 in range(0, tq, SELECT_ROWS)], axis=0)

    m_ref[...] = jnp.full(m_ref.shape, NEG_INF, F32)
    l_ref[...] = jnp.zeros(l_ref.shape, F32)
    acc_ref[...] = jnp.zeros(acc_ref.shape, F32)

    def attend_block(j, carry):
        off = pl.multiple_of(j * tq, tq)
        sel = (s_ref[:, pl.ds(off, tq)] >= thr) & (causal_diag | (j < i))
        bias = jnp.where(sel, 0.0, NEG_INF)
        for p2 in range(N_HEADS // 2):
            vblk = vb_ref[pl.ds(off, tq), p2 * LANES:(p2 + 1) * LANES]
            outs, alphas = [], []
            for h in (2 * p2, 2 * p2 + 1):
                kt = kt_ref[h * HEAD_DIM:h * HEAD_DIM + LANES, pl.ds(off, tq)]
                logits = jnp.dot(q_ref[:, h * LANES:(h + 1) * LANES], kt, preferred_element_type=F32) + bias
                p, alpha = _softmax_chunk(h, logits, m_ref, l_ref)
                outs.append(jnp.dot(p.astype(BF16), vblk, preferred_element_type=F32))
                alphas.append(alpha)
            sl = slice(p2 * LANES, (p2 + 1) * LANES)
            acc_ref[:, sl] = (acc_ref[:, sl] * jnp.where(lo_half, alphas[0], alphas[1])
                              + jnp.where(lo_half, outs[0], outs[1]))
        return carry

    lax.fori_loop(0, n_blocks, attend_block, 0)

    for p2 in range(N_HEADS // 2):
        sl = slice(p2 * LANES, (p2 + 1) * LANES)
        inv = jnp.where(lo_half, 1.0 / l_ref[:, 2 * p2:2 * p2 + 1], 1.0 / l_ref[:, 2 * p2 + 1:2 * p2 + 2])
        o_ref[:, sl] = (acc_ref[:, sl] * inv).astype(o_ref.dtype)


def _attn_prompt(q_pad, qi_pad, wi_rep, kt, vb, kit, tq):
    n = q_pad.shape[0]
    topk = min(TOPK_MAX, n // 4)
    row = lambda w: pl.BlockSpec((tq, w), lambda i: (i, 0))
    return pl.pallas_call(
        functools.partial(_attn_prompt_kernel, topk=topk),
        grid=(n // tq,),
        in_specs=[row(q_pad.shape[1]), row(qi_pad.shape[1]), row(wi_rep.shape[1]),
                  _const_spec(kt.shape), _const_spec(vb.shape), _const_spec(kit.shape)],
        out_specs=row(ATTN_DIM),
        out_shape=jax.ShapeDtypeStruct((n, ATTN_DIM), BF16),
        scratch_shapes=[pltpu.VMEM((tq, n), I32), pltpu.VMEM((tq, ATTN_DIM), F32),
                        pltpu.VMEM((tq, LANES), F32), pltpu.VMEM((tq, LANES), F32)],
        compiler_params=_cparams("arbitrary"),
        name="attn_prompt",
    )(q_pad, qi_pad, wi_rep, kt, vb, kit)


PAGE_GROUP = 4
TOK_ROWS = 8


def _attn_sample_kernel(pt_ref, qi_ref, wi_ref, q_ref, kin_ref, kn_ref, vn_ref, cik_hbm, ck_hbm, cv_hbm, o_ref,
                        s_ref, ibuf, kbuf, vbuf, sem, acc_ref, m_ref, l_ref, *, n_new, topk):
    b = pl.program_id(0)
    n_pages = pt_ref.shape[1]
    page = cik_hbm.shape[1]
    g = PAGE_GROUP
    n_groups = n_pages // g
    past = n_pages * page
    rows = TOK_ROWS

    def group_copies(src_hbm, buf, which, grp, slot):
        return [pltpu.make_async_copy(src_hbm.at[pt_ref[b, grp * g + j]], buf.at[slot, j], sem.at[which, slot])
                for j in range(g)]

    def start(src_hbm, buf, which, grp, slot):
        for c in group_copies(src_hbm, buf, which, grp, slot):
            c.start()

    def wait(src_hbm, buf, which, grp, slot):
        for c in group_copies(src_hbm, buf, which, grp, slot):
            c.wait()

    tok = lax.broadcasted_iota(I32, (rows, LANES), 0)
    lane = lax.broadcasted_iota(I32, (rows, LANES), 1)

    def head_sum(s):
        sc = jnp.zeros((rows, s.shape[1]), F32)
        for h in range(IDX_HEADS):
            w = wi_ref[0, h * rows:(h + 1) * rows, :]
            sc = sc + jnp.concatenate([w] * (s.shape[1] // LANES), axis=1) * jnp.maximum(s[h * rows:(h + 1) * rows], 0.0)
        return sc

    nt = (((1,), (1,)), ((), ()))

    start(cik_hbm, ibuf, 0, 0, 0)

    def score_group(grp, carry):
        slot = grp % 2
        wait(cik_hbm, ibuf, 0, grp, slot)

        @pl.when(grp + 1 < n_groups)
        def _():
            start(cik_hbm, ibuf, 0, grp + 1, 1 - slot)

        keys = ibuf[slot].reshape(g * page, IDX_DIM).astype(BF16)
        s = lax.dot_general(qi_ref[0], keys, nt, preferred_element_type=F32)
        s_ref[:, pl.ds(pl.multiple_of(grp * (g * page), g * page), g * page)] = _sortable_key(head_sum(s))
        return carry

    lax.fori_loop(0, n_groups, score_group, 0)
    s_new = lax.dot_general(qi_ref[0], kin_ref[0].astype(BF16), nt, preferred_element_type=F32)
    sc_new = jnp.where((lane < n_new) & (lane <= tok), head_sum(s_new), NEG_INF)
    s_ref[:, past:past + LANES] = _sortable_key(sc_new)

    thr = _select_threshold(s_ref, 0, rows, past // LANES + 1, topk, n_valid=n_new)[:, :1]

    m_ref[...] = jnp.full(m_ref.shape, NEG_INF, F32)
    l_ref[...] = jnp.zeros(l_ref.shape, F32)
    acc_ref[...] = jnp.zeros(acc_ref.shape, F32)

    def attend(h, kh, vh, sel):
        logits = lax.dot_general(q_ref[0, h], kh, nt, preferred_element_type=F32) + jnp.where(sel, 0.0, NEG_INF)
        p, alpha = _softmax_chunk(h, logits, m_ref, l_ref)
        acc_ref[h] = acc_ref[h] * alpha + jnp.dot(p.astype(BF16), vh, preferred_element_type=F32)

    start(ck_hbm, kbuf, 1, 0, 0)
    start(cv_hbm, vbuf, 2, 0, 0)

    def attend_group(grp, carry):
        slot = grp % 2
        wait(ck_hbm, kbuf, 1, grp, slot)
        wait(cv_hbm, vbuf, 2, grp, slot)

        @pl.when(grp + 1 < n_groups)
        def _():
            start(ck_hbm, kbuf, 1, grp + 1, 1 - slot)
            start(cv_hbm, vbuf, 2, grp + 1, 1 - slot)

        sel = s_ref[:, pl.ds(pl.multiple_of(grp * (g * page), g * page), g * page)] >= thr
        for h in range(N_HEADS):
            kh = kbuf[slot, :, :, h, :].reshape(g * page, HEAD_DIM).astype(BF16)
            vh = vbuf[slot, :, :, h, :].reshape(g * page, HEAD_DIM).astype(BF16)
            attend(h, kh, vh, sel)
        return carry

    lax.fori_loop(0, n_groups, attend_group, 0)
    sel_new = (s_ref[:, past:past + LANES] >= thr) & (lane < n_new) & (lane <= tok)
    for h in range(N_HEADS):
        attend(h, kn_ref[0, h].astype(BF16), vn_ref[0, h].astype(BF16), sel_new)

    for h in range(N_HEADS):
        o_ref[0, h] = (acc_ref[h] / l_ref[:, h:h + 1]).astype(o_ref.dtype)


def _attn_sample(page_table, qi_ht, wi_ht, q_ht, ki_new, k_new, v_new, cache_idx_k, cache_k, cache_v, n_new):
    nb, n_pages = page_table.shape
    page = cache_idx_k.shape[1]
    past = n_pages * page
    topk = min(TOPK_MAX, (past + n_new) // 4)
    g = PAGE_GROUP
    blk = lambda a: pl.BlockSpec((1,) + a.shape[1:], lambda b, pt: (b,) + (0,) * (a.ndim - 1))
    any_spec = pl.BlockSpec(memory_space=pl.ANY)
    return pl.pallas_call(
        functools.partial(_attn_sample_kernel, n_new=n_new, topk=topk),
        grid_spec=pltpu.PrefetchScalarGridSpec(
            num_scalar_prefetch=1,
            grid=(nb,),
            in_specs=[blk(qi_ht), blk(wi_ht), blk(q_ht), blk(ki_new), blk(k_new), blk(v_new),
                      any_spec, any_spec, any_spec],
            out_specs=pl.BlockSpec((1, N_HEADS, TOK_ROWS, HEAD_DIM), lambda b, pt: (b, 0, 0, 0)),
            scratch_shapes=[pltpu.VMEM((TOK_ROWS, past + LANES), I32),
                            pltpu.VMEM((2, g, page, IDX_DIM), F32),
                            pltpu.VMEM((2, g, page, N_HEADS, HEAD_DIM), F32),
                            pltpu.VMEM((2, g, page, N_HEADS, HEAD_DIM), F32),
                            pltpu.SemaphoreType.DMA((3, 2)),
                            pltpu.VMEM((N_HEADS, TOK_ROWS, HEAD_DIM), F32),
                            pltpu.VMEM((TOK_ROWS, LANES), F32), pltpu.VMEM((TOK_ROWS, LANES), F32)]),
        out_shape=jax.ShapeDtypeStruct((nb, N_HEADS, TOK_ROWS, HEAD_DIM), BF16),
        compiler_params=_cparams("arbitrary"),
        name="attn_sample",
    )(page_table, qi_ht, wi_ht, q_ht, ki_new, k_new, v_new, cache_idx_k, cache_k, cache_v)


CONV_HALO = 32


def _conv_prompt_kernel(a_ref, b_ref, dw_ref, db_ref, y_ref, cat_ref, *, width):
    tm = a_ref.shape[0]
    cat_ref[0:tm, :] = a_ref[...]
    cat_ref[tm:tm + CONV_HALO, :] = b_ref[...]
    lead = CONV_HALO - (width - 1)
    acc = jnp.zeros(y_ref.shape, F32) + db_ref[...]
    for w in range(width):
        acc = acc + dw_ref[w:w + 1, :] * cat_ref[lead + w:lead + w + tm, :]
    y_ref[...] = acc


def _conv_prompt(u, dw, db, tm):
    t, c = u.shape
    width = dw.shape[0]
    u_pad = jnp.concatenate([jnp.zeros((CONV_HALO, c), u.dtype), u], axis=0)
    return pl.pallas_call(
        functools.partial(_conv_prompt_kernel, width=width),
        grid=(t // tm,),
        in_specs=[pl.BlockSpec((tm, c), lambda i: (i, 0)),
                  pl.BlockSpec((CONV_HALO, c), lambda i: ((i + 1) * (tm // CONV_HALO), 0)),
                  _const_spec(dw.shape), _const_spec((1, c))],
        out_specs=pl.BlockSpec((tm, c), lambda i: (i, 0)),
        out_shape=jax.ShapeDtypeStruct((t, c), F32),
        scratch_shapes=[pltpu.VMEM((tm + CONV_HALO, c), F32)],
        compiler_params=_cparams("parallel"),
        name="conv_prompt",
    )(u_pad, u_pad, dw, db.reshape(1, c))


def _conv_sample_kernel(u_ref, dw_ref, db_ref, y_ref, *, width):
    for t in range(y_ref.shape[0]):
        acc = jnp.zeros(y_ref.shape[1:], F32) + db_ref[...]
        for w in range(width):
            acc = acc + dw_ref[w:w + 1, :] * u_ref[t + w]
        y_ref[t] = acc


def _conv_sample(u_tm, dw, db):
    rows, b, c = u_tm.shape
    width = dw.shape[0]
    t = rows - (width - 1)
    return pl.pallas_call(
        functools.partial(_conv_sample_kernel, width=width),
        out_shape=jax.ShapeDtypeStruct((t, b, c), F32),
        name="conv_sample",
    )(u_tm, dw, db.reshape(1, c))


def _merge_kernel(x_ref, attn_ref, y_ref, ga_ref, gc_ref, lng_ref, lnb_ref, woa_ref, wco_ref, wout_ref, o_ref):
    y = y_ref[...]
    mu = jnp.mean(y, axis=-1, keepdims=True)
    yc = y - mu
    var = jnp.mean(yc * yc, axis=-1, keepdims=True)
    z = yc * lax.rsqrt(var + EPS) * lng_ref[...] + lnb_ref[...]
    z = z * jax.nn.sigmoid(z)
    conv_y = jnp.dot(z.astype(BF16), wco_ref[...], preferred_element_type=F32)
    a = jnp.dot(attn_ref[...], woa_ref[...], preferred_element_type=F32)
    m = ga_ref[...] * a + gc_ref[...] * conv_y
    o_ref[...] = x_ref[...] + jnp.dot(m.astype(BF16), wout_ref[...], preferred_element_type=F32)


def _merge(x2d, attn, y, sga, sgc, ln_g, ln_b, woa, wco, wout, tm):
    n, d = x2d.shape
    c = y.shape[1]
    row = lambda w: pl.BlockSpec((tm, w), lambda i: (i, 0))
    return pl.pallas_call(
        _merge_kernel,
        grid=(n // tm,),
        in_specs=[row(d), row(attn.shape[1]), row(c), row(d), row(d), _const_spec((1, c)), _const_spec((1, c)),
                  _const_spec(woa.shape), _const_spec(wco.shape), _const_spec(wout.shape)],
        out_specs=row(d),
        out_shape=jax.ShapeDtypeStruct((n, d), F32),
        compiler_params=_cparams("parallel"),
        name="merge",
    )(x2d, attn, y, sga, sgc, ln_g.reshape(1, c), ln_b.reshape(1, c), woa, wco, wout)


def _extract_topk(s, k):
    rows = s.shape[0]
    ridx = lax.broadcasted_iota(I32, s.shape, 0)
    rank = jnp.full(s.shape, float(k), F32)
    vals = []
    for r in range(k):
        m = jnp.max(s, axis=0, keepdims=True)
        first = jnp.min(jnp.where(s == m, ridx, rows), axis=0, keepdims=True)
        hit = ridx == first
        rank = jnp.where(hit, float(r), rank)
        s = jnp.where(hit, NEG_INF, s)
        vals.append(m)
    return jnp.concatenate(vals, axis=0), rank


def _peer_candidates(k):
    return [(r1, r2) for r1 in range(k) for r2 in range(k) if (r1 + 1) * (r2 + 1) <= k]


def _peer_kernel(x_ref, g2_ref, wqt_ref, keys_ref, u_ref, vt_ref, o_ref,
                 ht_ref, qt_ref, s_ref, rank_ref, vals_ref, thr1_ref, e1_ref, e2_ref, acc_ref):
    e = pl.program_id(1)
    tn = x_ref.shape[0]
    n_keys = keys_ref.shape[1]
    n_heads = keys_ref.shape[0] // 2
    k = PEER_TOPK

    @pl.when(e == 0)
    def _select():
        x = x_ref[...]
        ms = jnp.mean(x * x, axis=-1, keepdims=True)
        h = x * lax.rsqrt(ms + EPS) * g2_ref[...]
        ht_ref[...] = h.T.astype(BF16)
        qt_ref[...] = jnp.dot(wqt_ref[...], ht_ref[...], preferred_element_type=F32).astype(BF16)
        half = keys_ref.shape[2]

        def half_scores(hc, carry):
            q_hc = qt_ref[pl.ds(pl.multiple_of(hc * half, half), half), :]
            s = jnp.dot(keys_ref[hc], q_hc, preferred_element_type=F32)
            s_ref[hc] = s
            vals, rank = _extract_topk(s, k)
            vals_ref[hc] = vals
            rank_ref[hc] = rank
            return carry

        lax.fori_loop(0, 2 * n_heads, half_scores, 0)

        pairs = _peer_candidates(k)
        pad = (-len(pairs)) % 8

        def head_weights(h, carry):
            v1 = vals_ref[2 * h]
            v2 = vals_ref[2 * h + 1]
            cand = jnp.concatenate([v1[r1:r1 + 1] + v2[r2:r2 + 1] for r1, r2 in pairs]
                                   + [jnp.full((pad, tn), NEG_INF, F32)], axis=0)
            _, crank = _extract_topk(cand, k)
            chosen = crank < float(k)
            ex1 = jnp.exp(v1 - v1[0:1])
            ex2 = jnp.exp(v2 - v2[0:1])
            zsum = jnp.zeros((1, tn), F32)
            counts = [jnp.zeros((1, tn), F32) for _ in range(k)]
            for row, (r1, r2) in enumerate(pairs):
                c = chosen[row:row + 1]
                zsum = zsum + jnp.where(c, ex1[r1:r1 + 1] * ex2[r2:r2 + 1], 0.0)
                counts[r1] = counts[r1] + jnp.where(c, 1.0, 0.0)
            rank1 = rank_ref[2 * h]
            thr1 = jnp.zeros((n_keys, tn), F32)
            for r1 in range(k):
                thr1 = jnp.where(rank1 == float(r1), counts[r1], thr1)
            thr1_ref[h] = thr1
            e1_ref[h] = jnp.exp(s_ref[2 * h] - v1[0:1]) / zsum
            e2_ref[h] = jnp.exp(s_ref[2 * h + 1] - v2[0:1])
            return carry

        lax.fori_loop(0, n_heads, head_weights, 0)
        acc_ref[...] = jnp.zeros(acc_ref.shape, F32)

    te = u_ref.shape[0]
    act = jnp.dot(u_ref[...], ht_ref[...], preferred_element_type=F32)
    act = jax.nn.gelu(act)
    blocks = []
    for ii in range(te // n_keys):
        i = e * (te // n_keys) + ii
        w = jnp.zeros((n_keys, tn), F32)
        for h in range(n_heads):
            t1 = thr1_ref[h, pl.ds(i, 1), :]
            w = w + e1_ref[h, pl.ds(i, 1), :] * jnp.where(rank_ref[2 * h + 1] < t1, e2_ref[h], 0.0)
        blocks.append((w * act[ii * n_keys:(ii + 1) * n_keys]).astype(BF16))
    p = jnp.concatenate(blocks, axis=0)
    acc_ref[...] += jnp.dot(vt_ref[...], p, preferred_element_type=F32)

    @pl.when(e == pl.num_programs(1) - 1)
    def _finish():
        o_ref[...] = x_ref[...] + acc_ref[...].T


def _peer(x1, ln2_g, wqt, keys, u_bf, vt_bf, tn, te):
    n, d = x1.shape
    n_exp = u_bf.shape[0]
    hc, n_keys, half = keys.shape
    k = PEER_TOPK
    return pl.pallas_call(
        _peer_kernel,
        grid=(n // tn, n_exp // te),
        in_specs=[pl.BlockSpec((tn, d), lambda i, e: (i, 0)), _const_spec((1, d)), _const_spec(wqt.shape),
                  _const_spec(keys.shape), pl.BlockSpec((te, d), lambda i, e: (e, 0)),
                  pl.BlockSpec((d, te), lambda i, e: (0, e))],
        out_specs=pl.BlockSpec((tn, d), lambda i, e: (i, 0)),
        out_shape=jax.ShapeDtypeStruct((n, d), F32),
        scratch_shapes=[pltpu.VMEM((d, tn), BF16), pltpu.VMEM((hc * half, tn), BF16),
                        pltpu.VMEM((hc, n_keys, tn), F32), pltpu.VMEM((hc, n_keys, tn), F32),
                        pltpu.VMEM((hc, k, tn), F32), pltpu.VMEM((hc // 2, n_keys, tn), F32),
                        pltpu.VMEM((hc // 2, n_keys, tn), F32), pltpu.VMEM((hc // 2, n_keys, tn), F32),
                        pltpu.VMEM((d, tn), F32)],
        compiler_params=_cparams("parallel", "arbitrary"),
        name="peer",
    )(x1, ln2_g.reshape(1, d), wqt, keys, u_bf, vt_bf)


def kernel(x_prompt, x_sample, cache_k, cache_v, cache_idx_k, state_conv, page_table, ln1_g, w_in, q_norm_g, k_norm_g, w_o_attn, conv_dw_w, conv_dw_b, conv_ln_g, conv_ln_b, w_conv_out, w_out, ln2_g, peer_w_q, peer_sub_keys, peer_u, peer_v):
    bp, t, d = x_prompt.shape
    db, ds, _ = x_sample.shape
    assert bp == 1 and ds <= TOK_ROWS
    hist = conv_dw_w.shape[0] - 1
    H, dh = N_HEADS, HEAD_DIM

    w_packed = _pack_w_in(w_in, d)
    woa, wco, wout = w_o_attn.astype(BF16), w_conv_out.astype(BF16), w_out.astype(BF16)
    wqt = peer_w_q.reshape(d, -1).T.astype(BF16)
    sub_keys = peer_sub_keys.reshape(-1, peer_sub_keys.shape[2], peer_sub_keys.shape[3]).astype(BF16)
    u_bf = peer_u.astype(BF16)
    vt_bf = peer_v.T.astype(BF16)

    def tail(x1, tn):
        return _peer(x1, ln2_g, wqt, sub_keys, u_bf, vt_bf, tn, PEER_EXPERT_TILE)

    qs, k, kt, v, vb, qib, ki, kit, wi, u, ga, gc = _proj(x_prompt[0], ln1_g, w_packed, q_norm_g, k_norm_g, ROW_TILE)
    attn = _attn_prompt(qs, qib, wi, kt, vb, kit, Q_TILE)
    y = _conv_prompt(u, conv_dw_w, conv_dw_b, ROW_TILE)
    x1 = _merge(x_prompt[0], attn, y, ga, gc, conv_ln_g, conv_ln_b, woa, wco, wout, ROW_TILE)
    y_prompt = tail(x1, PEER_TOKEN_TILE).reshape(1, t, d)
    conv_p_state = jnp.concatenate([jnp.zeros((hist, u.shape[1]), F32), u], axis=0)[-hist:].reshape(1, hist, -1)

    n_s = db * ds
    qs_s, k_s, _, v_s, _, qib_s, ki_s, _, wi_s, u_s, ga_s, gc_s = _proj(
        x_sample.reshape(n_s, d), ln1_g, w_packed, q_norm_g, k_norm_g, n_s)

    def per_head(a, width):
        a = a.reshape(db, ds, H, width).transpose(0, 2, 1, 3)
        return jnp.pad(a, ((0, 0), (0, 0), (0, TOK_ROWS - ds), (0, 0)))

    qi_ht = per_head(qib_s, LANES)[..., :IDX_DIM].reshape(db, H * TOK_ROWS, IDX_DIM)
    wi_ht = per_head(wi_s, LANES).reshape(db, H * TOK_ROWS, LANES)
    q_ht = per_head(qs_s, LANES)[..., :dh]
    pad_keys = lambda a: jnp.pad(a, ((0, 0),) * (a.ndim - 2) + ((0, LANES - ds), (0, 0)))
    ki_new = pad_keys(ki_s.reshape(db, ds, IDX_DIM))
    k_new = pad_keys(k_s.reshape(db, ds, H, dh).transpose(0, 2, 1, 3))
    v_new = pad_keys(v_s.reshape(db, ds, H, dh).transpose(0, 2, 1, 3))
    attn_s = _attn_sample(page_table, qi_ht, wi_ht, q_ht, ki_new, k_new, v_new, cache_idx_k, cache_k, cache_v, ds)
    attn_s = attn_s[:, :, :ds].transpose(0, 2, 1, 3).reshape(n_s, H * dh)
    u_pad_s = jnp.concatenate([state_conv, u_s.reshape(db, ds, -1)], axis=1)
    y_s = _conv_sample(u_pad_s.transpose(1, 0, 2), conv_dw_w, conv_dw_b).transpose(1, 0, 2).reshape(n_s, -1)
    x1_s = _merge(x_sample.reshape(n_s, d), attn_s, y_s, ga_s, gc_s, conv_ln_g, conv_ln_b, woa, wco, wout, n_s)
    y_sample = tail(x1_s, n_s).reshape(db, ds, d)

    return (y_prompt, y_sample,
            k.reshape(1, t, H, dh), v.reshape(1, t, H, dh), ki.reshape(1, t, IDX_DIM), conv_p_state,
            k_s.reshape(db, ds, H, dh), v_s.reshape(db, ds, H, dh), ki_s.reshape(db, ds, IDX_DIM),
            u_pad_s[:, -hist:])
```

```python
import functools

import jax
import jax.numpy as jnp
import numpy as np
from jax import lax
from jax.experimental import pallas as pl
from jax.experimental.pallas import tpu as pltpu

F32 = jnp.float32
BF16 = jnp.bfloat16
I32 = jnp.int32

EPS = 1e-6
N_HEADS = 8
HEAD_DIM = 64
ATTN_DIM = N_HEADS * HEAD_DIM
IDX_HEADS = 8
IDX_DIM = 64
TOPK_MAX = 256
CONV_DIM = 512
PEER_TOPK = 16
LANES = 128
SELECT_ROWS = 128
ROW_TILE = 256
Q_TILE = 256
KEY_BLOCK = 512
PEER_TOKEN_TILE = 512
PEER_EXPERT_TILE = 512
PEER_TOKEN_CHUNK = 256
VMEM_LIMIT = 56 * 1024 * 1024

NEG_INF = float("-inf")
INT_MIN = -(2 ** 31)


def _cparams(*sem):
    return pltpu.CompilerParams(dimension_semantics=sem, vmem_limit_bytes=VMEM_LIMIT)


def _const_spec(shape):
    nd = len(shape)
    return pl.BlockSpec(shape, lambda *_: (0,) * nd, pipeline_mode=pl.Buffered(1))


def _group_rms(x, gsum_ref, gain):
    x2 = x * x
    hi = x2.astype(BF16)
    lo = (x2 - hi.astype(F32)).astype(BF16)
    ss = (jnp.dot(hi, gsum_ref[...], preferred_element_type=F32)
          + jnp.dot(lo, gsum_ref[...], preferred_element_type=F32))
    return x * lax.rsqrt(ss * (1.0 / HEAD_DIM) + EPS) * gain


def _pad_heads(x):
    lo_half = lax.broadcasted_iota(I32, (x.shape[0], LANES), 1) < HEAD_DIM
    outs = []
    for p in range(x.shape[1] // LANES):
        slab = x[:, p * LANES:(p + 1) * LANES]
        outs.append(jnp.where(lo_half, slab, 0.0))
        outs.append(jnp.where(lo_half, pltpu.roll(slab, HEAD_DIM, 1), 0.0))
    return jnp.concatenate(outs, axis=1)


def _proj_kernel(x_ref, g1_ref, w_ref, qg_ref, kg_ref, gsum_ref,
                 q_ref, k_ref, kt_ref, v_ref, vb_ref, qi_ref, ki_ref, kit_ref, wi_ref, u_ref, ga_ref, gc_ref):
    x = x_ref[...]
    tm = x.shape[0]
    ms = jnp.mean(x * x, axis=-1, keepdims=True)
    h = (x * lax.rsqrt(ms + EPS) * g1_ref[...]).astype(BF16)

    def seg(lo, width):
        return jnp.dot(h, w_ref[:, lo:lo + width], preferred_element_type=F32)

    A = ATTN_DIM
    q = _group_rms(seg(0, A), gsum_ref, qg_ref[...])
    q_ref[...] = _pad_heads(q * (HEAD_DIM ** -0.5)).astype(BF16)
    k = _group_rms(seg(A, A), gsum_ref, kg_ref[...])
    k_ref[...] = k
    kt_ref[...] = k.T.astype(BF16)
    v = seg(2 * A, A)
    v_ref[...] = v
    vb_ref[...] = v.astype(BF16)
    qi_ref[...] = _pad_heads(seg(3 * A, A)).astype(BF16)
    o = 4 * A
    ki = seg(o, LANES)
    ki_ref[...] = ki[:, :IDX_DIM]
    kit_ref[...] = ki.T[:IDX_DIM].astype(BF16)
    wi = seg(o + LANES, LANES)
    wi_ref[...] = jnp.concatenate(
        [jnp.broadcast_to(wi[:, hh:hh + 1], (tm, LANES)) for hh in range(IDX_HEADS)], axis=1)
    o += 2 * LANES
    a = seg(o, CONV_DIM)
    b = seg(o + CONV_DIM, CONV_DIM)
    u_ref[...] = a * jax.nn.sigmoid(b)
    o += 2 * CONV_DIM
    d = x.shape[-1]
    ga_ref[...] = jax.nn.sigmoid(seg(o, d))
    gc_ref[...] = jax.nn.sigmoid(seg(o + d, d))


def _pack_w_in(w_in, d):
    A = ATTN_DIM
    o = 4 * A
    pad = lambda w: jnp.pad(w, ((0, 0), (0, LANES - w.shape[1])))
    parts = [w_in[:, :o], pad(w_in[:, o:o + IDX_DIM]), pad(w_in[:, o + IDX_DIM:o + IDX_DIM + IDX_HEADS]),
             w_in[:, o + IDX_DIM + IDX_HEADS:]]
    return jnp.concatenate(parts, axis=1).astype(BF16)


def _proj(x2d, ln1_g, w_packed, q_norm_g, k_norm_g, tm):
    n, d = x2d.shape
    A = ATTN_DIM
    wcols = w_packed.shape[1]
    gsum = jnp.asarray(np.kron(np.eye(N_HEADS), np.ones((HEAD_DIM, HEAD_DIM))), BF16)
    row = lambda w: pl.BlockSpec((tm, w), lambda i: (i, 0))
    colT = lambda r: pl.BlockSpec((r, tm), lambda i: (0, i))
    sds = jax.ShapeDtypeStruct
    return pl.pallas_call(
        _proj_kernel,
        grid=(n // tm,),
        in_specs=[row(d), _const_spec((1, d)), _const_spec((d, wcols)), _const_spec((1, A)), _const_spec((1, A)),
                  _const_spec((A, A))],
        out_specs=[row(2 * A), row(A), colT(A), row(A), row(A), row(2 * A), row(IDX_DIM), colT(IDX_DIM),
                   row(IDX_HEADS * LANES), row(CONV_DIM), row(d), row(d)],
        out_shape=[sds((n, 2 * A), BF16), sds((n, A), F32), sds((A, n), BF16), sds((n, A), F32),
                   sds((n, A), BF16), sds((n, 2 * A), BF16), sds((n, IDX_DIM), F32), sds((IDX_DIM, n), BF16),
                   sds((n, IDX_HEADS * LANES), F32), sds((n, CONV_DIM), F32), sds((n, d), F32), sds((n, d), F32)],
        compiler_params=_cparams("parallel"),
        name="proj",
    )(x2d, ln1_g.reshape(1, d), w_packed, jnp.tile(q_norm_g, N_HEADS).reshape(1, A),
      jnp.tile(k_norm_g, N_HEADS).reshape(1, A), gsum)


def _sortable_key(score):
    b = lax.bitcast_convert_type(jnp.where(score == 0.0, 0.0, score), I32)
    return b ^ ((b >> 31) & 0x7FFFFFFF)


def _count_rows(s_ref, r0, rows, n_chunks, unroll, pred):
    col0 = lax.broadcasted_iota(I32, (rows, LANES), 1)

    def body(c, acc):
        for u in range(unroll):
            off = pl.multiple_of((c * unroll + u) * LANES, LANES)
            acc = acc + jnp.where(pred(s_ref[r0:r0 + rows, pl.ds(off, LANES)], col0 + off), 1, 0)
        return acc

    acc = lax.fori_loop(0, n_chunks // unroll, body, jnp.zeros((rows, LANES), I32))
    return jnp.broadcast_to(jnp.sum(acc, axis=1, keepdims=True), (rows, LANES))


def _select_threshold(s_ref, r0, rows, n_chunks, k, n_valid=None, unroll=1):
    count = functools.partial(_count_rows, s_ref, r0, rows, n_chunks, unroll)

    zero = jnp.zeros((rows, LANES), I32)
    sign_prefix = jnp.where(count(lambda keys, _: keys >= zero) >= k, zero, zero + INT_MIN)

    def bit_step(it, prefix):
        cand = prefix | jnp.left_shift(jnp.int32(1), 30 - it)
        cnt = count(lambda keys, _: keys >= cand)
        return jnp.where(cnt >= k, cand, prefix)

    thr = lax.fori_loop(0, 31, bit_step, sign_prefix)
    n_ge = count(lambda keys, _: keys >= thr)
    if n_valid is not None:
        n_ge = jnp.where(lax.broadcasted_iota(I32, (rows, LANES), 0) < n_valid, n_ge, 0)

    @pl.when(jnp.max(n_ge) > k)
    def _demote_surplus_ties():
        need = k - count(lambda keys, _: keys > thr)

        def idx_step(it, prefix):
            cand = prefix | jnp.left_shift(jnp.int32(1), 14 - it)
            cnt = count(lambda keys, idx: jnp.where(keys == thr, idx, cand) < cand)
            return jnp.where(cnt < need, cand, prefix)

        last = lax.fori_loop(0, 15, idx_step, jnp.zeros((rows, LANES), I32))
        col0 = lax.broadcasted_iota(I32, (rows, LANES), 1)

        def demote(c, carry):
            off = pl.multiple_of(c * LANES, LANES)
            keys = s_ref[r0:r0 + rows, pl.ds(off, LANES)]
            surplus = jnp.where(keys == thr, col0 + off, 0) > last
            s_ref[r0:r0 + rows, pl.ds(off, LANES)] = jnp.where(surplus, keys - 1, keys)
            return carry

        lax.fori_loop(0, n_chunks, demote, 0)

    return thr


def _softmax_chunk(h, logits, m_ref, l_ref):
    m_old = m_ref[:, h:h + 1]
    m_new = jnp.maximum(m_old, jnp.max(logits, axis=1, keepdims=True))
    m_safe = jnp.where(m_new == NEG_INF, 0.0, m_new)
    p = jnp.exp(logits - m_safe)
    alpha = jnp.exp(m_old - m_safe)
    l_ref[:, h:h + 1] = l_ref[:, h:h + 1] * alpha + jnp.sum(p, axis=1, keepdims=True)
    m_ref[:, h:h + 1] = m_new
    return p, alpha


def _attn_prompt_kernel(q_ref, qi_ref, wi_ref, kit_ref, shift_ref, kt_hbm, vb_hbm, o_ref,
                        s_ref, acc_ref, m_ref, l_ref, kbuf, vbuf, sem, *, topk, online):
    i = pl.program_id(0)
    tq = q_ref.shape[0]
    tk = KEY_BLOCK
    n_blocks = (i * tq + tq + tk - 1) // tk
    rel = (lax.broadcasted_iota(I32, (tq, tk), 1) - lax.broadcasted_iota(I32, (tq, tk), 0)) - i * tq
    lo_half = lax.broadcasted_iota(I32, (tq, LANES), 1) < HEAD_DIM

    def kv_copies(j, slot):
        off = pl.multiple_of(j * tk, tk)
        return (pltpu.make_async_copy(kt_hbm.at[:, pl.ds(off, tk)], kbuf.at[slot], sem.at[0, slot]),
                pltpu.make_async_copy(vb_hbm.at[pl.ds(off, tk), :], vbuf.at[slot], sem.at[1, slot]))

    for c in kv_copies(0, 0):
        c.start()

    def score_block(j, carry):
        off = pl.multiple_of(j * tk, tk)
        kit = kit_ref[:, pl.ds(off, tk)]
        sc = jnp.zeros((tq, tk), F32)
        for h in range(IDX_HEADS):
            s = jnp.dot(qi_ref[:, h * LANES:h * LANES + IDX_DIM], kit, preferred_element_type=F32)
            w = wi_ref[:, h * LANES:(h + 1) * LANES]
            sc = sc + jnp.concatenate([w] * (tk // LANES), axis=1) * jnp.maximum(s, 0.0)
        sc = jnp.where(rel + j * tk <= 0, sc, NEG_INF)
        s_ref[:, pl.ds(off, tk)] = _sortable_key(sc)
        return carry

    lax.fori_loop(0, n_blocks, score_block, 0)

    n_chunks = n_blocks * (tk // LANES)
    thr = jnp.concatenate([_select_threshold(s_ref, r0, SELECT_ROWS, n_chunks, topk, unroll=tk // LANES)[:, :1]
                           for r0 in range(0, tq, SELECT_ROWS)], axis=0)

    if online:
        m_ref[...] = jnp.full(m_ref.shape, NEG_INF, F32)
    l_ref[...] = jnp.zeros(l_ref.shape, F32)
    acc_ref[...] = jnp.zeros(acc_ref.shape, F32)

    def attend_block(j, carry):
        slot = j % 2
        for c in kv_copies(j, slot):
            c.wait()

        @pl.when(j + 1 < n_blocks)
        def _():
            for c in kv_copies(j + 1, 1 - slot):
                c.start()

        off = pl.multiple_of(j * tk, tk)
        keys = s_ref[:, pl.ds(off, tk)]
        chosen = 0.0 if online else -shift_ref[0:1, 0:1]
        bias = jnp.where(rel + j * tk <= 0, jnp.where(keys >= thr, chosen, NEG_INF), NEG_INF)
        for p2 in range(N_HEADS // 2):
            vblk = vbuf[slot, :, p2 * LANES:(p2 + 1) * LANES]
            sl = slice(p2 * LANES, (p2 + 1) * LANES)
            outs, alphas = [], []
            for h in (2 * p2, 2 * p2 + 1):
                kt = kbuf[slot, h * HEAD_DIM:(h + 1) * HEAD_DIM, :]
                logits = jnp.dot(q_ref[:, h * LANES:h * LANES + HEAD_DIM], kt, preferred_element_type=F32) + bias
                if online:
                    p, alpha = _softmax_chunk(h, logits, m_ref, l_ref)
                    alphas.append(alpha)
                else:
                    p = jnp.exp(logits)
                    l_ref[h] += sum(p[:, c * LANES:(c + 1) * LANES] for c in range(tk // LANES))
                outs.append(jnp.dot(p.astype(BF16), vblk, preferred_element_type=F32))
            new = jnp.where(lo_half, outs[0], outs[1])
            if online:
                acc_ref[:, sl] = acc_ref[:, sl] * jnp.where(lo_half, alphas[0], alphas[1]) + new
            else:
                acc_ref[:, sl] += new
        return carry

    lax.fori_loop(0, n_blocks, attend_block, 0)

    for p2 in range(N_HEADS // 2):
        sl = slice(p2 * LANES, (p2 + 1) * LANES)
        if online:
            l0, l1 = l_ref[:, 2 * p2:2 * p2 + 1], l_ref[:, 2 * p2 + 1:2 * p2 + 2]
        else:
            l0, l1 = (jnp.sum(l_ref[h], axis=1, keepdims=True) for h in (2 * p2, 2 * p2 + 1))
        o_ref[:, sl] = (acc_ref[:, sl] * jnp.where(lo_half, 1.0 / l0, 1.0 / l1)).astype(o_ref.dtype)


SAFE_SHIFT = 40.0


def _attn_prompt(q_pad, qi_pad, wi_rep, kt, vb, kit, logit_bound, tq):
    n = q_pad.shape[0]
    assert n % KEY_BLOCK == 0 and KEY_BLOCK % tq == 0
    topk = min(TOPK_MAX, n // 4)
    row = lambda w: pl.BlockSpec((tq, w), lambda i: (i, 0))
    any_spec = pl.BlockSpec(memory_space=pl.ANY)
    shift = jnp.full((1, LANES), logit_bound, F32)

    def call(online):
        stats = (pltpu.VMEM((tq, LANES), F32) if online else pltpu.VMEM((N_HEADS, tq, LANES), F32))
        return pl.pallas_call(
            functools.partial(_attn_prompt_kernel, topk=topk, online=online),
            grid=(n // tq,),
            in_specs=[row(q_pad.shape[1]), row(qi_pad.shape[1]), row(wi_rep.shape[1]),
                      _const_spec(kit.shape), _const_spec(shift.shape), any_spec, any_spec],
            out_specs=row(ATTN_DIM),
            out_shape=jax.ShapeDtypeStruct((n, ATTN_DIM), BF16),
            scratch_shapes=[pltpu.VMEM((tq, n), I32), pltpu.VMEM((tq, ATTN_DIM), F32),
                            pltpu.VMEM((tq, LANES), F32), stats,
                            pltpu.VMEM((2, ATTN_DIM, KEY_BLOCK), BF16), pltpu.VMEM((2, KEY_BLOCK, ATTN_DIM), BF16),
                            pltpu.SemaphoreType.DMA((2, 2))],
            compiler_params=_cparams("arbitrary"),
            name="attn_prompt_online" if online else "attn_prompt",
        )(q_pad, qi_pad, wi_rep, kit, shift, kt, vb)

    return lax.cond(logit_bound <= SAFE_SHIFT, lambda: call(False), lambda: call(True))


PAGE_GROUP = 8
TOK_ROWS = 8
SAMPLE_UNROLL = 12


def _sample_tail_chunks(n_pages, page):
    cached = n_pages * page // LANES
    return (-cached) % SAMPLE_UNROLL or SAMPLE_UNROLL


def _attn_sample_kernel(pt_ref, qi_ref, wi_ref, q_ref, kin_ref, kn_ref, vn_ref, cik_hbm, ck_hbm, cv_hbm, o_ref,
                        s_ref, ibuf, kbuf, vbuf, sem, acc_ref, m_ref, l_ref, *, n_new, topk):
    b = pl.program_id(0)
    n_pages = pt_ref.shape[1]
    page = cik_hbm.shape[2]
    g = PAGE_GROUP
    n_groups = n_pages // g
    past = n_pages * page
    rows = TOK_ROWS

    def group_copies(src_hbm, buf, which, grp, slot):
        return [pltpu.make_async_copy(src_hbm.at[pt_ref[b, grp * g + j]], buf.at[slot, j], sem.at[which, slot])
                for j in range(g)]

    def start(src_hbm, buf, which, grp, slot):
        for c in group_copies(src_hbm, buf, which, grp, slot):
            c.start()

    def wait(src_hbm, buf, which, grp, slot):
        for c in group_copies(src_hbm, buf, which, grp, slot):
            c.wait()

    tok = lax.broadcasted_iota(I32, (rows, LANES), 0)
    lane = lax.broadcasted_iota(I32, (rows, LANES), 1)

    def head_sum(s):
        sc = jnp.zeros((rows, s.shape[1]), F32)
        for h in range(IDX_HEADS):
            w = wi_ref[0, h * rows:(h + 1) * rows, :]
            sc = sc + jnp.concatenate([w] * (s.shape[1] // LANES), axis=1) * jnp.maximum(s[h * rows:(h + 1) * rows], 0.0)
        return sc

    nt = (((1,), (1,)), ((), ()))

    def pages_bf16(buf, slot, *lead):
        return jnp.concatenate([buf[(slot, j) + lead] for j in range(g)], axis=1).astype(BF16)

    start(cik_hbm, ibuf, 0, 0, 0)

    def score_group(grp, carry):
        slot = grp % 2
        wait(cik_hbm, ibuf, 0, grp, slot)

        @pl.when(grp + 1 < n_groups)
        def _():
            start(cik_hbm, ibuf, 0, grp + 1, 1 - slot)

        s = jnp.dot(qi_ref[0], pages_bf16(ibuf, slot), preferred_element_type=F32)
        s_ref[:, pl.ds(pl.multiple_of(grp * (g * page), g * page), g * page)] = _sortable_key(head_sum(s))
        return carry

    lax.fori_loop(0, n_groups, score_group, 0)
    s_new = jnp.dot(qi_ref[0], kin_ref[0].astype(BF16), preferred_element_type=F32)
    sc_new = jnp.where((lane < n_new) & (lane <= tok), head_sum(s_new), NEG_INF)
    s_ref[:, past:past + LANES] = _sortable_key(sc_new)
    tail = _sample_tail_chunks(n_pages, page)
    if tail > 1:
        s_ref[:, past + LANES:past + tail * LANES] = _sortable_key(jnp.full((rows, (tail - 1) * LANES), NEG_INF, F32))

    thr = _select_threshold(s_ref, 0, rows, past // LANES + tail, topk, n_valid=n_new, unroll=SAMPLE_UNROLL)[:, :1]

    m_ref[...] = jnp.full(m_ref.shape, NEG_INF, F32)
    l_ref[...] = jnp.zeros(l_ref.shape, F32)
    acc_ref[...] = jnp.zeros(acc_ref.shape, F32)

    def attend(kht, vht, sel):
        bias = jnp.where(sel, 0.0, NEG_INF)
        logits = jnp.concatenate(
            [jnp.dot(q_ref[0, h], kht(h), preferred_element_type=F32) + bias for h in range(N_HEADS)], axis=0)
        m_old = m_ref[...]
        m_new = jnp.maximum(m_old, jnp.max(logits, axis=1, keepdims=True))
        m_safe = jnp.where(m_new == NEG_INF, 0.0, m_new)
        p = jnp.exp(logits - m_safe).astype(BF16)
        alpha = jnp.exp(m_old - m_safe)
        l_ref[...] = l_ref[...] * alpha + jnp.sum(p.astype(F32), axis=1, keepdims=True)
        m_ref[...] = m_new
        pv = jnp.concatenate(
            [lax.dot_general(p[h * rows:(h + 1) * rows], vht(h), nt, preferred_element_type=F32)
             for h in range(N_HEADS)], axis=0)
        acc_ref[...] = acc_ref[...] * alpha + pv

    start(ck_hbm, kbuf, 1, 0, 0)
    start(cv_hbm, vbuf, 2, 0, 0)

    def attend_group(grp, carry):
        slot = grp % 2
        wait(ck_hbm, kbuf, 1, grp, slot)
        wait(cv_hbm, vbuf, 2, grp, slot)

        @pl.when(grp + 1 < n_groups)
        def _():
            start(ck_hbm, kbuf, 1, grp + 1, 1 - slot)
            start(cv_hbm, vbuf, 2, grp + 1, 1 - slot)

        sel = s_ref[:, pl.ds(pl.multiple_of(grp * (g * page), g * page), g * page)] >= thr
        attend(lambda h: pages_bf16(kbuf, slot, h), lambda h: pages_bf16(vbuf, slot, h), sel)
        return carry

    lax.fori_loop(0, n_groups, attend_group, 0)
    sel_new = (s_ref[:, past:past + LANES] >= thr) & (lane < n_new) & (lane <= tok)
    attend(lambda h: kn_ref[0, h].astype(BF16), lambda h: vn_ref[0, h].astype(BF16), sel_new)

    out = acc_ref[...] / l_ref[...]
    for h in range(N_HEADS):
        o_ref[0, h] = out[h * rows:(h + 1) * rows].astype(o_ref.dtype)


def _attn_sample(page_table, qi_ht, wi_ht, q_ht, ki_new, k_new, v_new, cache_idx_k, cache_k, cache_v, n_new):
    nb, n_pages = page_table.shape
    page = cache_idx_k.shape[2]
    past = n_pages * page
    topk = min(TOPK_MAX, (past + n_new) // 4)
    g = PAGE_GROUP
    blk = lambda a: pl.BlockSpec((1,) + a.shape[1:], lambda b, pt: (b,) + (0,) * (a.ndim - 1))
    any_spec = pl.BlockSpec(memory_space=pl.ANY)
    return pl.pallas_call(
        functools.partial(_attn_sample_kernel, n_new=n_new, topk=topk),
        grid_spec=pltpu.PrefetchScalarGridSpec(
            num_scalar_prefetch=1,
            grid=(nb,),
            in_specs=[blk(qi_ht), blk(wi_ht), blk(q_ht), blk(ki_new), blk(k_new), blk(v_new),
                      any_spec, any_spec, any_spec],
            out_specs=pl.BlockSpec((1, N_HEADS, TOK_ROWS, HEAD_DIM), lambda b, pt: (b, 0, 0, 0)),
            scratch_shapes=[pltpu.VMEM((TOK_ROWS, past + _sample_tail_chunks(n_pages, page) * LANES), I32),
                            pltpu.VMEM((2, g, IDX_DIM, page), F32),
                            pltpu.VMEM((2, g, N_HEADS, HEAD_DIM, page), F32),
                            pltpu.VMEM((2, g, N_HEADS, HEAD_DIM, page), F32),
                            pltpu.SemaphoreType.DMA((3, 2)),
                            pltpu.VMEM((N_HEADS * TOK_ROWS, HEAD_DIM), F32),
                            pltpu.VMEM((N_HEADS * TOK_ROWS, 1), F32), pltpu.VMEM((N_HEADS * TOK_ROWS, 1), F32)]),
        out_shape=jax.ShapeDtypeStruct((nb, N_HEADS, TOK_ROWS, HEAD_DIM), BF16),
        compiler_params=_cparams("arbitrary"),
        name="attn_sample",
    )(page_table, qi_ht, wi_ht, q_ht, ki_new, k_new, v_new, cache_idx_k, cache_k, cache_v)


CONV_HALO = 32


def _conv_prompt_kernel(a_ref, b_ref, dw_ref, db_ref, y_ref, cat_ref, *, width):
    tm = a_ref.shape[0]
    cat_ref[0:tm, :] = a_ref[...]
    cat_ref[tm:tm + CONV_HALO, :] = b_ref[...]
    lead = CONV_HALO - (width - 1)
    acc = jnp.zeros(y_ref.shape, F32) + db_ref[...]
    for w in range(width):
        acc = acc + dw_ref[w:w + 1, :] * cat_ref[lead + w:lead + w + tm, :]
    y_ref[...] = acc


def _conv_prompt(u, dw, db, tm):
    t, c = u.shape
    width = dw.shape[0]
    u_pad = jnp.concatenate([jnp.zeros((CONV_HALO, c), u.dtype), u], axis=0)
    return pl.pallas_call(
        functools.partial(_conv_prompt_kernel, width=width),
        grid=(t // tm,),
        in_specs=[pl.BlockSpec((tm, c), lambda i: (i, 0)),
                  pl.BlockSpec((CONV_HALO, c), lambda i: ((i + 1) * (tm // CONV_HALO), 0)),
                  _const_spec(dw.shape), _const_spec((1, c))],
        out_specs=pl.BlockSpec((tm, c), lambda i: (i, 0)),
        out_shape=jax.ShapeDtypeStruct((t, c), F32),
        scratch_shapes=[pltpu.VMEM((tm + CONV_HALO, c), F32)],
        compiler_params=_cparams("parallel"),
        name="conv_prompt",
    )(u_pad, u_pad, dw, db.reshape(1, c))


def _conv_sample_kernel(u_ref, dw_ref, db_ref, y_ref, *, width):
    for t in range(y_ref.shape[0]):
        acc = jnp.zeros(y_ref.shape[1:], F32) + db_ref[...]
        for w in range(width):
            acc = acc + dw_ref[w:w + 1, :] * u_ref[t + w]
        y_ref[t] = acc


def _conv_sample(u_tm, dw, db):
    rows, b, c = u_tm.shape
    width = dw.shape[0]
    t = rows - (width - 1)
    return pl.pallas_call(
        functools.partial(_conv_sample_kernel, width=width),
        out_shape=jax.ShapeDtypeStruct((t, b, c), F32),
        name="conv_sample",
    )(u_tm, dw, db.reshape(1, c))


def _merge_kernel(x_ref, attn_ref, y_ref, ga_ref, gc_ref, lng_ref, lnb_ref, woa_ref, wco_ref, wout_ref, o_ref):
    y = y_ref[...]
    mu = jnp.mean(y, axis=-1, keepdims=True)
    yc = y - mu
    var = jnp.mean(yc * yc, axis=-1, keepdims=True)
    z = yc * lax.rsqrt(var + EPS) * lng_ref[...] + lnb_ref[...]
    z = z * jax.nn.sigmoid(z)
    conv_y = jnp.dot(z.astype(BF16), wco_ref[...], preferred_element_type=F32)
    a = jnp.dot(attn_ref[...], woa_ref[...], preferred_element_type=F32)
    m = ga_ref[...] * a + gc_ref[...] * conv_y
    o_ref[...] = x_ref[...] + jnp.dot(m.astype(BF16), wout_ref[...], preferred_element_type=F32)


def _merge(x2d, attn, y, sga, sgc, ln_g, ln_b, woa, wco, wout, tm):
    n, d = x2d.shape
    c = y.shape[1]
    row = lambda w: pl.BlockSpec((tm, w), lambda i: (i, 0))
    return pl.pallas_call(
        _merge_kernel,
        grid=(n // tm,),
        in_specs=[row(d), row(attn.shape[1]), row(c), row(d), row(d), _const_spec((1, c)), _const_spec((1, c)),
                  _const_spec(woa.shape), _const_spec(wco.shape), _const_spec(wout.shape)],
        out_specs=row(d),
        out_shape=jax.ShapeDtypeStruct((n, d), F32),
        compiler_params=_cparams("parallel"),
        name="merge",
    )(x2d, attn, y, sga, sgc, ln_g.reshape(1, c), ln_b.reshape(1, c), woa, wco, wout)


def _extract_topk(s, k):
    rows = s.shape[0]
    ridx = lax.broadcasted_iota(I32, s.shape, 0)
    rank = jnp.full(s.shape, float(k), F32)
    vals = []
    for r in range(k):
        m = jnp.max(s, axis=0, keepdims=True)
        first = jnp.min(jnp.where(s == m, ridx, rows), axis=0, keepdims=True)
        hit = ridx == first
        rank = jnp.where(hit, float(r), rank)
        s = jnp.where(hit, NEG_INF, s)
        vals.append(m)
    return jnp.concatenate(vals, axis=0), rank


def _peer_candidates(k):
    return [(r1, r2) for r1 in range(k) for r2 in range(k) if (r1 + 1) * (r2 + 1) <= k]


BF16_ROWS = 16


def _peer_kernel(x_ref, g2_ref, wqt_ref, keys_ref, u_ref, vt_ref, o_ref,
                 ht_ref, qt_ref, s_ref, rank_ref, vals_ref, thr1_ref, e1_ref, rank2_ref, e2_ref, p_ref, acc_ref):
    e = pl.program_id(1)
    n_e = pl.num_programs(1) - 1
    tn = x_ref.shape[0]
    n_keys = keys_ref.shape[1]
    n_heads = keys_ref.shape[0] // 2
    k = PEER_TOPK
    packed = (n_keys // BF16_ROWS, BF16_ROWS, tn)

    @pl.when(e == 0)
    def _select():
        x = x_ref[...]
        ms = jnp.mean(x * x, axis=-1, keepdims=True)
        h = x * lax.rsqrt(ms + EPS) * g2_ref[...]
        ht_ref[...] = h.T.astype(BF16)
        qt_ref[...] = jnp.dot(wqt_ref[...], ht_ref[...], preferred_element_type=F32).astype(BF16)
        half = keys_ref.shape[2]

        def half_scores(hc, carry):
            q_hc = qt_ref[pl.ds(pl.multiple_of(hc * half, half), half), :]
            s = jnp.dot(keys_ref[hc], q_hc, preferred_element_type=F32)
            s_ref[hc] = s
            vals, rank = _extract_topk(s, k)
            vals_ref[hc] = vals
            rank_ref[hc] = rank
            return carry

        lax.fori_loop(0, 2 * n_heads, half_scores, 0)

        pairs = _peer_candidates(k)
        pad = (-len(pairs)) % 8

        def head_weights(h, carry):
            v1 = vals_ref[2 * h]
            v2 = vals_ref[2 * h + 1]
            cand = jnp.concatenate([v1[r1:r1 + 1] + v2[r2:r2 + 1] for r1, r2 in pairs]
                                   + [jnp.full((pad, tn), NEG_INF, F32)], axis=0)
            _, crank = _extract_topk(cand, k)
            chosen = crank < float(k)
            ex1 = jnp.exp(v1 - v1[0:1])
            ex2 = jnp.exp(v2 - v2[0:1])
            zsum = jnp.zeros((1, tn), F32)
            counts = [jnp.zeros((1, tn), F32) for _ in range(k)]
            for row, (r1, r2) in enumerate(pairs):
                c = chosen[row:row + 1]
                zsum = zsum + jnp.where(c, ex1[r1:r1 + 1] * ex2[r2:r2 + 1], 0.0)
                counts[r1] = counts[r1] + jnp.where(c, 1.0, 0.0)
            rank1 = rank_ref[2 * h]
            thr1 = jnp.zeros((n_keys, tn), F32)
            for r1 in range(k):
                thr1 = jnp.where(rank1 == float(r1), counts[r1], thr1)
            thr1_ref[h] = thr1
            e1_ref[h] = jnp.exp(s_ref[2 * h] - v1[0:1]) / zsum
            rank2_ref[h] = rank_ref[2 * h + 1].reshape(packed).astype(BF16)
            e2_ref[h] = jnp.exp(s_ref[2 * h + 1] - v2[0:1]).reshape(packed).astype(BF16)
            return carry

        lax.fori_loop(0, n_heads, head_weights, 0)
        acc_ref[...] = jnp.zeros(acc_ref.shape, F32)
        p_ref[1] = jnp.zeros(p_ref.shape[1:], BF16)

    te = u_ref.shape[0]
    i0 = jnp.minimum(e, n_e - 1) * (te // n_keys)

    def step(new, old):
        tc = min(tn, PEER_TOKEN_CHUNK)
        packed_c = packed[:2] + (tc,)
        cols = [slice(c * tc, (c + 1) * tc) for c in range(tn // tc)]
        for cs in cols:
            ws = []
            for ii in range(te // n_keys):
                w = jnp.zeros(packed_c, BF16)
                for h in range(n_heads):
                    row = lambda ref: jnp.broadcast_to(ref[h, pl.ds(i0 + ii, 1), cs], (BF16_ROWS, tc)).astype(BF16)[None]
                    w = w + row(e1_ref) * jnp.where(rank2_ref[h, :, :, cs] < row(thr1_ref), e2_ref[h, :, :, cs],
                                                    jnp.zeros((), BF16))
                ws.append(w)
            acc_ref[:, cs] += jnp.dot(vt_ref[...], p_ref[old, :, cs], preferred_element_type=F32)
            act = jax.nn.gelu(jnp.dot(u_ref[...], ht_ref[:, cs], preferred_element_type=F32))
            for ii, w in enumerate(ws):
                g = act[ii * n_keys:(ii + 1) * n_keys].reshape(packed_c).astype(BF16)
                p_ref[new, ii * n_keys:(ii + 1) * n_keys, cs] = (w * g).reshape(n_keys, tc)

    for parity in (0, 1):
        pl.when(e % 2 == parity)(functools.partial(step, parity, 1 - parity))

    @pl.when(e == n_e)
    def _finish():
        o_ref[...] = x_ref[...] + acc_ref[...].T


def _peer(x1, ln2_g, wqt, keys, u_bf, vt_bf, tn, te):
    n, d = x1.shape
    n_exp = u_bf.shape[0]
    hc, n_keys, half = keys.shape
    k = PEER_TOPK
    n_e = n_exp // te
    n_h = hc // 2
    packed = (n_h, n_keys // BF16_ROWS, BF16_ROWS, tn)
    return pl.pallas_call(
        _peer_kernel,
        grid=(n // tn, n_e + 1),
        in_specs=[pl.BlockSpec((tn, d), lambda i, e: (i, 0)), _const_spec((1, d)), _const_spec(wqt.shape),
                  _const_spec(keys.shape), pl.BlockSpec((te, d), lambda i, e: (jnp.minimum(e, n_e - 1), 0)),
                  pl.BlockSpec((d, te), lambda i, e: (0, jnp.maximum(e - 1, 0)))],
        out_specs=pl.BlockSpec((tn, d), lambda i, e: (i, 0)),
        out_shape=jax.ShapeDtypeStruct((n, d), F32),
        scratch_shapes=[pltpu.VMEM((d, tn), BF16), pltpu.VMEM((hc * half, tn), BF16),
                        pltpu.VMEM((hc, n_keys, tn), F32), pltpu.VMEM((hc, n_keys, tn), F32),
                        pltpu.VMEM((hc, k, tn), F32), pltpu.VMEM((n_h, n_keys, tn), F32),
                        pltpu.VMEM((n_h, n_keys, tn), F32), pltpu.VMEM(packed, BF16), pltpu.VMEM(packed, BF16),
                        pltpu.VMEM((2, te, tn), BF16), pltpu.VMEM((d, tn), F32)],
        compiler_params=_cparams("parallel", "arbitrary"),
        name="peer",
    )(x1, ln2_g.reshape(1, d), wqt, keys, u_bf, vt_bf)


def kernel(x_prompt, x_sample, cache_k, cache_v, cache_idx_k, state_conv, page_table, ln1_g, w_in, q_norm_g, k_norm_g, w_o_attn, conv_dw_w, conv_dw_b, conv_ln_g, conv_ln_b, w_conv_out, w_out, ln2_g, peer_w_q, peer_sub_keys, peer_u, peer_v):
    bp, t, d = x_prompt.shape
    db, ds, _ = x_sample.shape
    assert bp == 1 and ds <= TOK_ROWS
    hist = conv_dw_w.shape[0] - 1
    H, dh = N_HEADS, HEAD_DIM

    w_packed = _pack_w_in(w_in, d)
    woa, wco, wout = w_o_attn.astype(BF16), w_conv_out.astype(BF16), w_out.astype(BF16)
    wqt = peer_w_q.reshape(d, -1).T.astype(BF16)
    sub_keys = peer_sub_keys.reshape(-1, peer_sub_keys.shape[2], peer_sub_keys.shape[3]).astype(BF16)
    u_bf = peer_u.astype(BF16)
    vt_bf = peer_v.T.astype(BF16)

    def tail(x1, tn):
        return _peer(x1, ln2_g, wqt, sub_keys, u_bf, vt_bf, tn, PEER_EXPERT_TILE)

    qs, k, kt, v, vb, qib, ki, kit, wi, u, ga, gc = _proj(x_prompt[0], ln1_g, w_packed, q_norm_g, k_norm_g, ROW_TILE)
    logit_bound = 1.02 * (dh ** 0.5) * jnp.max(jnp.abs(q_norm_g)) * jnp.max(jnp.abs(k_norm_g))
    attn = _attn_prompt(qs, qib, wi, kt, vb, kit, logit_bound, Q_TILE)
    y = _conv_prompt(u, conv_dw_w, conv_dw_b, ROW_TILE)
    x1 = _merge(x_prompt[0], attn, y, ga, gc, conv_ln_g, conv_ln_b, woa, wco, wout, ROW_TILE)
    y_prompt = tail(x1, PEER_TOKEN_TILE).reshape(1, t, d)
    conv_p_state = jnp.concatenate([jnp.zeros((hist, u.shape[1]), F32), u], axis=0)[-hist:].reshape(1, hist, -1)

    n_s = db * ds
    qs_s, k_s, _, v_s, _, qib_s, ki_s, _, wi_s, u_s, ga_s, gc_s = _proj(
        x_sample.reshape(n_s, d), ln1_g, w_packed, q_norm_g, k_norm_g, n_s)

    def per_head(a, width):
        a = a.reshape(db, ds, H, width).transpose(0, 2, 1, 3)
        return jnp.pad(a, ((0, 0), (0, 0), (0, TOK_ROWS - ds), (0, 0)))

    qi_ht = per_head(qib_s, LANES)[..., :IDX_DIM].reshape(db, H * TOK_ROWS, IDX_DIM)
    wi_ht = per_head(wi_s, LANES).reshape(db, H * TOK_ROWS, LANES)
    q_ht = per_head(qs_s, LANES)[..., :dh]
    pad_keys = lambda a: jnp.pad(a, ((0, 0),) * (a.ndim - 1) + ((0, LANES - ds),))
    ki_new = pad_keys(ki_s.reshape(db, ds, IDX_DIM).transpose(0, 2, 1))
    k_new = pad_keys(k_s.reshape(db, ds, H, dh).transpose(0, 2, 3, 1))
    v_new = pad_keys(v_s.reshape(db, ds, H, dh).transpose(0, 2, 3, 1))
    attn_s = _attn_sample(page_table, qi_ht, wi_ht, q_ht, ki_new, k_new, v_new, cache_idx_k.transpose(0, 2, 1),
                          cache_k.transpose(0, 2, 3, 1), cache_v.transpose(0, 2, 3, 1), ds)
    attn_s = attn_s[:, :, :ds].transpose(0, 2, 1, 3).reshape(n_s, H * dh)
    u_pad_s = jnp.concatenate([state_conv, u_s.reshape(db, ds, -1)], axis=1)
    y_s = _conv_sample(u_pad_s.transpose(1, 0, 2), conv_dw_w, conv_dw_b).transpose(1, 0, 2).reshape(n_s, -1)
    x1_s = _merge(x_sample.reshape(n_s, d), attn_s, y_s, ga_s, gc_s, conv_ln_g, conv_ln_b, woa, wco, wout, n_s)
    y_sample = tail(x1_s, n_s).reshape(db, ds, d)

    return (y_prompt, y_sample,
            k.reshape(1, t, H, dh), v.reshape(1, t, H, dh), ki.reshape(1, t, IDX_DIM), conv_p_state,
            k_s.reshape(db, ds, H, dh), v_s.reshape(db, ds, H, dh), ki_s.reshape(db, ds, IDX_DIM),
            u_pad_s[:, -hist:])
```

```python
import functools

import jax
import jax.numpy as jnp
import numpy as np
from jax import lax
from jax.experimental import pallas as pl
from jax.experimental.pallas import tpu as pltpu

F32 = jnp.float32
BF16 = jnp.bfloat16
I32 = jnp.int32

EPS = 1e-6
N_HEADS = 8
HEAD_DIM = 64
ATTN_DIM = N_HEADS * HEAD_DIM
IDX_HEADS = 8
IDX_DIM = 64
TOPK_MAX = 256
CONV_DIM = 512
PEER_TOPK = 16
LANES = 128
SELECT_ROWS = 128
ROW_TILE = 256
Q_TILE = 256
KEY_BLOCK = 512
PEER_TOKEN_TILE = 512
PEER_EXPERT_TILE = 512
PEER_TOKEN_CHUNK = 256
VMEM_LIMIT = 56 * 1024 * 1024

NEG_INF = float("-inf")
INT_MIN = -(2 ** 31)


def _cparams(*sem):
    return pltpu.CompilerParams(dimension_semantics=sem, vmem_limit_bytes=VMEM_LIMIT)


def _const_spec(shape):
    nd = len(shape)
    return pl.BlockSpec(shape, lambda *_: (0,) * nd, pipeline_mode=pl.Buffered(1))


def _group_rms(x, gsum_ref, gain):
    x2 = x * x
    hi = x2.astype(BF16)
    lo = (x2 - hi.astype(F32)).astype(BF16)
    ss = (jnp.dot(hi, gsum_ref[...], preferred_element_type=F32)
          + jnp.dot(lo, gsum_ref[...], preferred_element_type=F32))
    return x * lax.rsqrt(ss * (1.0 / HEAD_DIM) + EPS) * gain


def _pad_heads(x):
    lo_half = lax.broadcasted_iota(I32, (x.shape[0], LANES), 1) < HEAD_DIM
    outs = []
    for p in range(x.shape[1] // LANES):
        slab = x[:, p * LANES:(p + 1) * LANES]
        outs.append(jnp.where(lo_half, slab, 0.0))
        outs.append(jnp.where(lo_half, pltpu.roll(slab, HEAD_DIM, 1), 0.0))
    return jnp.concatenate(outs, axis=1)


def _proj_kernel(x_ref, g1_ref, w_ref, qg_ref, kg_ref, gsum_ref,
                 q_ref, k_ref, kt_ref, v_ref, vb_ref, qi_ref, ki_ref, kit_ref, wi_ref, u_ref, ga_ref, gc_ref):
    x = x_ref[...]
    tm = x.shape[0]
    ms = jnp.mean(x * x, axis=-1, keepdims=True)
    h = (x * lax.rsqrt(ms + EPS) * g1_ref[...]).astype(BF16)

    def seg(lo, width):
        return jnp.dot(h, w_ref[:, lo:lo + width], preferred_element_type=F32)

    A = ATTN_DIM
    q = _group_rms(seg(0, A), gsum_ref, qg_ref[...])
    q_ref[...] = _pad_heads(q * (HEAD_DIM ** -0.5)).astype(BF16)
    k = _group_rms(seg(A, A), gsum_ref, kg_ref[...])
    k_ref[...] = k
    kt_ref[...] = k.T.astype(BF16)
    v = seg(2 * A, A)
    v_ref[...] = v
    vb_ref[...] = v.astype(BF16)
    qi_ref[...] = _pad_heads(seg(3 * A, A)).astype(BF16)
    o = 4 * A
    ki = seg(o, LANES)
    ki_ref[...] = ki[:, :IDX_DIM]
    kit_ref[...] = ki.T[:IDX_DIM].astype(BF16)
    wi = seg(o + LANES, LANES)
    wi_ref[...] = jnp.concatenate(
        [jnp.broadcast_to(wi[:, hh:hh + 1], (tm, LANES)) for hh in range(IDX_HEADS)], axis=1)
    o += 2 * LANES
    a = seg(o, CONV_DIM)
    b = seg(o + CONV_DIM, CONV_DIM)
    u_ref[...] = a * jax.nn.sigmoid(b)
    o += 2 * CONV_DIM
    d = x.shape[-1]
    ga_ref[...] = jax.nn.sigmoid(seg(o, d))
    gc_ref[...] = jax.nn.sigmoid(seg(o + d, d))


def _pack_w_in(w_in, d):
    A = ATTN_DIM
    o = 4 * A
    pad = lambda w: jnp.pad(w, ((0, 0), (0, LANES - w.shape[1])))
    parts = [w_in[:, :o], pad(w_in[:, o:o + IDX_DIM]), pad(w_in[:, o + IDX_DIM:o + IDX_DIM + IDX_HEADS]),
             w_in[:, o + IDX_DIM + IDX_HEADS:]]
    return jnp.concatenate(parts, axis=1).astype(BF16)


def _proj(x2d, ln1_g, w_packed, q_norm_g, k_norm_g, tm):
    n, d = x2d.shape
    A = ATTN_DIM
    wcols = w_packed.shape[1]
    gsum = jnp.asarray(np.kron(np.eye(N_HEADS), np.ones((HEAD_DIM, HEAD_DIM))), BF16)
    row = lambda w: pl.BlockSpec((tm, w), lambda i: (i, 0))
    colT = lambda r: pl.BlockSpec((r, tm), lambda i: (0, i))
    sds = jax.ShapeDtypeStruct
    return pl.pallas_call(
        _proj_kernel,
        grid=(n // tm,),
        in_specs=[row(d), _const_spec((1, d)), _const_spec((d, wcols)), _const_spec((1, A)), _const_spec((1, A)),
                  _const_spec((A, A))],
        out_specs=[row(2 * A), row(A), colT(A), row(A), row(A), row(2 * A), row(IDX_DIM), colT(IDX_DIM),
                   row(IDX_HEADS * LANES), row(CONV_DIM), row(d), row(d)],
        out_shape=[sds((n, 2 * A), BF16), sds((n, A), F32), sds((A, n), BF16), sds((n, A), F32),
                   sds((n, A), BF16), sds((n, 2 * A), BF16), sds((n, IDX_DIM), F32), sds((IDX_DIM, n), BF16),
                   sds((n, IDX_HEADS * LANES), F32), sds((n, CONV_DIM), F32), sds((n, d), F32), sds((n, d), F32)],
        compiler_params=_cparams("parallel"),
        name="proj",
    )(x2d, ln1_g.reshape(1, d), w_packed, jnp.tile(q_norm_g, N_HEADS).reshape(1, A),
      jnp.tile(k_norm_g, N_HEADS).reshape(1, A), gsum)


def _sortable_key(score):
    b = lax.bitcast_convert_type(jnp.where(score == 0.0, 0.0, score), I32)
    return b ^ ((b >> 31) & 0x7FFFFFFF)


def _count_rows(s_ref, r0, rows, n_chunks, unroll, pred):
    col0 = lax.broadcasted_iota(I32, (rows, LANES), 1)

    def body(c, acc):
        for u in range(unroll):
            off = pl.multiple_of((c * unroll + u) * LANES, LANES)
            acc = acc + jnp.where(pred(s_ref[r0:r0 + rows, pl.ds(off, LANES)], col0 + off), 1, 0)
        return acc

    acc = lax.fori_loop(0, n_chunks // unroll, body, jnp.zeros((rows, LANES), I32))
    return jnp.broadcast_to(jnp.sum(acc, axis=1, keepdims=True), (rows, LANES))


def _demote_surplus_ties(s_ref, r0, rows, n_chunks, k, unroll, thr):
    count = functools.partial(_count_rows, s_ref, r0, rows, n_chunks, unroll)
    need = k - count(lambda keys, _: keys > thr)

    def idx_step(it, prefix):
        cand = prefix | jnp.left_shift(jnp.int32(1), 14 - it)
        cnt = count(lambda keys, idx: jnp.where(keys == thr, idx, cand) < cand)
        return jnp.where(cnt < need, cand, prefix)

    last = lax.fori_loop(0, 15, idx_step, jnp.zeros((rows, LANES), I32))
    col0 = lax.broadcasted_iota(I32, (rows, LANES), 1)

    def demote(c, carry):
        off = pl.multiple_of(c * LANES, LANES)
        keys = s_ref[r0:r0 + rows, pl.ds(off, LANES)]
        surplus = jnp.where(keys == thr, col0 + off, 0) > last
        s_ref[r0:r0 + rows, pl.ds(off, LANES)] = jnp.where(surplus, keys - 1, keys)
        return carry

    lax.fori_loop(0, n_chunks, demote, 0)


def _select_threshold(s_ref, r0, rows, n_chunks, k, n_valid=None, unroll=1):
    count = functools.partial(_count_rows, s_ref, r0, rows, n_chunks, unroll)

    zero = jnp.zeros((rows, LANES), I32)
    sign_prefix = jnp.where(count(lambda keys, _: keys >= zero) >= k, zero, zero + INT_MIN)

    def bit_step(it, prefix):
        cand = prefix | jnp.left_shift(jnp.int32(1), 30 - it)
        cnt = count(lambda keys, _: keys >= cand)
        return jnp.where(cnt >= k, cand, prefix)

    thr = lax.fori_loop(0, 31, bit_step, sign_prefix)
    n_ge = count(lambda keys, _: keys >= thr)
    if n_valid is not None:
        n_ge = jnp.where(lax.broadcasted_iota(I32, (rows, LANES), 0) < n_valid, n_ge, 0)

    pl.when(jnp.max(n_ge) > k)(functools.partial(_demote_surplus_ties, s_ref, r0, rows, n_chunks, k, unroll, thr))
    return thr


I16 = jnp.int16
I16_MIN = -(2 ** 15)


def _count16(ref, n_chunks, unroll, thr, strict=False):
    rows = ref.shape[0]
    thr16 = jnp.broadcast_to(thr, (rows, LANES)).astype(I16)
    one, zero = jnp.ones((), I16), jnp.zeros((), I16)

    def body(c, acc):
        for u in range(unroll):
            blk = ref[:, pl.ds(pl.multiple_of((c * unroll + u) * LANES, LANES), LANES)]
            acc = acc + jnp.where(blk > thr16 if strict else blk >= thr16, one, zero)
        return acc

    acc = lax.fori_loop(0, n_chunks // unroll, body, jnp.zeros((rows, LANES), I16))
    return jnp.sum(acc.astype(I32), axis=1, keepdims=True)


def _radix16(ref, n_chunks, unroll, need):
    rows = ref.shape[0]
    zero = jnp.zeros((rows, 1), I32)
    prefix = jnp.where(_count16(ref, n_chunks, unroll, zero) >= need, zero, zero + I16_MIN)

    def bit_step(it, prefix):
        cand = prefix | jnp.left_shift(jnp.int32(1), 14 - it)
        return jnp.where(_count16(ref, n_chunks, unroll, cand) >= need, cand, prefix)

    return lax.fori_loop(0, 15, bit_step, prefix)


def _select_threshold_split(s_ref, hi_ref, lo_ref, n_chunks, k, unroll):
    rows = s_ref.shape[0]
    t_hi = _radix16(hi_ref, n_chunks, unroll, k)
    need_lo = k - _count16(hi_ref, n_chunks, unroll, t_hi, strict=True)
    t_hi16 = jnp.broadcast_to(t_hi, (rows, LANES)).astype(I16)

    def low_halves(c, carry):
        for u in range(unroll):
            off = pl.multiple_of((c * unroll + u) * LANES, LANES)
            lo = (s_ref[:, pl.ds(off, LANES)] ^ 0x8000).astype(I16)
            lo_ref[:, pl.ds(off, LANES)] = jnp.where(hi_ref[:, pl.ds(off, LANES)] == t_hi16, lo, jnp.full((), I16_MIN, I16))
        return carry

    lax.fori_loop(0, n_chunks // unroll, low_halves, 0)
    t_lo = _radix16(lo_ref, n_chunks, unroll, need_lo)
    thr = jnp.left_shift(t_hi, 16) | ((t_lo ^ 0x8000) & 0xFFFF)
    n_ge = (k - need_lo) + _count16(lo_ref, n_chunks, unroll, t_lo)

    @pl.when(jnp.max(n_ge) > k)
    def _surplus_ties():
        for r0 in range(0, rows, SELECT_ROWS):
            _demote_surplus_ties(s_ref, r0, SELECT_ROWS, n_chunks, k, unroll,
                                 jnp.broadcast_to(thr[r0:r0 + SELECT_ROWS], (SELECT_ROWS, LANES)))

    return thr


def _softmax_chunk(h, logits, m_ref, l_ref):
    m_old = m_ref[:, h:h + 1]
    m_new = jnp.maximum(m_old, jnp.max(logits, axis=1, keepdims=True))
    m_safe = jnp.where(m_new == NEG_INF, 0.0, m_new)
    p = jnp.exp(logits - m_safe)
    alpha = jnp.exp(m_old - m_safe)
    l_ref[:, h:h + 1] = l_ref[:, h:h + 1] * alpha + jnp.sum(p, axis=1, keepdims=True)
    m_ref[:, h:h + 1] = m_new
    return p, alpha


def _attn_prompt_kernel(q_ref, qi_ref, wi_ref, kit_ref, shift_ref, kt_hbm, vb_hbm, o_ref,
                        s_ref, hi_ref, lo_ref, acc_ref, m_ref, l_ref, kbuf, vbuf, sem, *, topk, online):
    i = pl.program_id(0)
    tq = q_ref.shape[0]
    tk = KEY_BLOCK
    n_blocks = (i * tq + tq + tk - 1) // tk
    rel = (lax.broadcasted_iota(I32, (tq, tk), 1) - lax.broadcasted_iota(I32, (tq, tk), 0)) - i * tq
    lo_half = lax.broadcasted_iota(I32, (tq, LANES), 1) < HEAD_DIM

    def kv_copies(j, slot):
        off = pl.multiple_of(j * tk, tk)
        return (pltpu.make_async_copy(kt_hbm.at[:, pl.ds(off, tk)], kbuf.at[slot], sem.at[0, slot]),
                pltpu.make_async_copy(vb_hbm.at[pl.ds(off, tk), :], vbuf.at[slot], sem.at[1, slot]))

    for c in kv_copies(0, 0):
        c.start()

    def score_block(j, carry):
        off = pl.multiple_of(j * tk, tk)
        kit = kit_ref[:, pl.ds(off, tk)]
        sc = jnp.zeros((tq, tk), F32)
        for h in range(IDX_HEADS):
            s = jnp.dot(qi_ref[:, h * LANES:h * LANES + IDX_DIM], kit, preferred_element_type=F32)
            w = wi_ref[:, h * LANES:(h + 1) * LANES]
            sc = sc + jnp.concatenate([w] * (tk // LANES), axis=1) * jnp.maximum(s, 0.0)
        sc = jnp.where(rel + j * tk <= 0, sc, NEG_INF)
        keys = _sortable_key(sc)
        s_ref[:, pl.ds(off, tk)] = keys
        hi_ref[:, pl.ds(off, tk)] = (keys >> 16).astype(I16)
        return carry

    lax.fori_loop(0, n_blocks, score_block, 0)

    thr = _select_threshold_split(s_ref, hi_ref, lo_ref, n_blocks * (tk // LANES), topk, tk // LANES)

    if online:
        m_ref[...] = jnp.full(m_ref.shape, NEG_INF, F32)
    l_ref[...] = jnp.zeros(l_ref.shape, F32)
    acc_ref[...] = jnp.zeros(acc_ref.shape, F32)

    def attend_block(j, carry):
        slot = j % 2
        for c in kv_copies(j, slot):
            c.wait()

        @pl.when(j + 1 < n_blocks)
        def _():
            for c in kv_copies(j + 1, 1 - slot):
                c.start()

        off = pl.multiple_of(j * tk, tk)
        keys = s_ref[:, pl.ds(off, tk)]
        chosen = 0.0 if online else -shift_ref[0:1, 0:1]
        bias = jnp.where(rel + j * tk <= 0, jnp.where(keys >= thr, chosen, NEG_INF), NEG_INF)
        for p2 in range(N_HEADS // 2):
            vblk = vbuf[slot, :, p2 * LANES:(p2 + 1) * LANES]
            sl = slice(p2 * LANES, (p2 + 1) * LANES)
            outs, alphas = [], []
            for h in (2 * p2, 2 * p2 + 1):
                kt = kbuf[slot, h * HEAD_DIM:(h + 1) * HEAD_DIM, :]
                logits = jnp.dot(q_ref[:, h * LANES:h * LANES + HEAD_DIM], kt, preferred_element_type=F32) + bias
                if online:
                    p, alpha = _softmax_chunk(h, logits, m_ref, l_ref)
                    alphas.append(alpha)
                else:
                    p = jnp.exp(logits)
                    l_ref[h] += sum(p[:, c * LANES:(c + 1) * LANES] for c in range(tk // LANES))
                outs.append(jnp.dot(p.astype(BF16), vblk, preferred_element_type=F32))
            new = jnp.where(lo_half, outs[0], outs[1])
            if online:
                acc_ref[:, sl] = acc_ref[:, sl] * jnp.where(lo_half, alphas[0], alphas[1]) + new
            else:
                acc_ref[:, sl] += new
        return carry

    lax.fori_loop(0, n_blocks, attend_block, 0)

    for p2 in range(N_HEADS // 2):
        sl = slice(p2 * LANES, (p2 + 1) * LANES)
        if online:
            l0, l1 = l_ref[:, 2 * p2:2 * p2 + 1], l_ref[:, 2 * p2 + 1:2 * p2 + 2]
        else:
            l0, l1 = (jnp.sum(l_ref[h], axis=1, keepdims=True) for h in (2 * p2, 2 * p2 + 1))
        o_ref[:, sl] = (acc_ref[:, sl] * jnp.where(lo_half, 1.0 / l0, 1.0 / l1)).astype(o_ref.dtype)


SAFE_SHIFT = 40.0


def _attn_prompt(q_pad, qi_pad, wi_rep, kt, vb, kit, logit_bound, tq):
    n = q_pad.shape[0]
    assert n % KEY_BLOCK == 0 and KEY_BLOCK % tq == 0
    topk = min(TOPK_MAX, n // 4)
    row = lambda w: pl.BlockSpec((tq, w), lambda i: (i, 0))
    any_spec = pl.BlockSpec(memory_space=pl.ANY)
    shift = jnp.full((1, LANES), logit_bound, F32)

    def call(online):
        stats = (pltpu.VMEM((tq, LANES), F32) if online else pltpu.VMEM((N_HEADS, tq, LANES), F32))
        return pl.pallas_call(
            functools.partial(_attn_prompt_kernel, topk=topk, online=online),
            grid=(n // tq,),
            in_specs=[row(q_pad.shape[1]), row(qi_pad.shape[1]), row(wi_rep.shape[1]),
                      _const_spec(kit.shape), _const_spec(shift.shape), any_spec, any_spec],
            out_specs=row(ATTN_DIM),
            out_shape=jax.ShapeDtypeStruct((n, ATTN_DIM), BF16),
            scratch_shapes=[pltpu.VMEM((tq, n), I32), pltpu.VMEM((tq, n), I16), pltpu.VMEM((tq, n), I16),
                            pltpu.VMEM((tq, ATTN_DIM), F32),
                            pltpu.VMEM((tq, LANES), F32), stats,
                            pltpu.VMEM((2, ATTN_DIM, KEY_BLOCK), BF16), pltpu.VMEM((2, KEY_BLOCK, ATTN_DIM), BF16),
                            pltpu.SemaphoreType.DMA((2, 2))],
            compiler_params=_cparams("arbitrary"),
            name="attn_prompt_online" if online else "attn_prompt",
        )(q_pad, qi_pad, wi_rep, kit, shift, kt, vb)

    return lax.cond(logit_bound <= SAFE_SHIFT, lambda: call(False), lambda: call(True))


PAGE_GROUP = 8
PAGE_SLOTS = 4
TOK_ROWS = 8
SAMPLE_UNROLL = 12


def _sample_tail_chunks(n_pages, page):
    cached = n_pages * page // LANES
    return (-cached) % SAMPLE_UNROLL or SAMPLE_UNROLL


def _attn_sample_kernel(pt_ref, qi_ref, wi_ref, q_ref, kin_ref, kn_ref, vn_ref, cik_hbm, ck_hbm, cv_hbm, o_ref,
                        s_ref, ibuf, kbuf, vbuf, sem, acc_ref, m_ref, l_ref, *, n_new, topk):
    b = pl.program_id(0)
    n_pages = pt_ref.shape[1]
    page = cik_hbm.shape[2]
    g = PAGE_GROUP
    n_groups = n_pages // g
    past = n_pages * page
    rows = TOK_ROWS

    def group_copies(src_hbm, buf, which, grp, slot):
        return [pltpu.make_async_copy(src_hbm.at[pt_ref[b, grp * g + j]], buf.at[slot, j], sem.at[which, slot])
                for j in range(g)]

    def start(src_hbm, buf, which, grp, slot):
        for c in group_copies(src_hbm, buf, which, grp, slot):
            c.start()

    def wait(src_hbm, buf, which, grp, slot):
        for c in group_copies(src_hbm, buf, which, grp, slot):
            c.wait()

    tok = lax.broadcasted_iota(I32, (rows, LANES), 0)
    lane = lax.broadcasted_iota(I32, (rows, LANES), 1)

    def head_sum(s):
        sc = jnp.zeros((rows, s.shape[1]), F32)
        for h in range(IDX_HEADS):
            w = wi_ref[0, h * rows:(h + 1) * rows, :]
            sc = sc + jnp.concatenate([w] * (s.shape[1] // LANES), axis=1) * jnp.maximum(s[h * rows:(h + 1) * rows], 0.0)
        return sc

    nt = (((1,), (1,)), ((), ()))

    def pages_bf16(buf, slot, *lead):
        return jnp.concatenate([buf[(slot, j) + lead] for j in range(g)], axis=1).astype(BF16)

    ahead = PAGE_SLOTS - 1

    def prefetch_first(*streams):
        for grp in range(min(ahead, n_groups)):
            for src_hbm, buf, which in streams:
                start(src_hbm, buf, which, grp, grp)

    def advance(grp, *streams):
        slot = grp % PAGE_SLOTS
        for src_hbm, buf, which in streams:
            wait(src_hbm, buf, which, grp, slot)

        @pl.when(grp + ahead < n_groups)
        def _():
            for src_hbm, buf, which in streams:
                start(src_hbm, buf, which, grp + ahead, (grp + ahead) % PAGE_SLOTS)

        return slot

    idx_stream = (cik_hbm, ibuf, 0)
    kv_streams = ((ck_hbm, kbuf, 1), (cv_hbm, vbuf, 2))
    prefetch_first(idx_stream)
    prefetch_first(*kv_streams)

    def score_group(grp, carry):
        slot = advance(grp, idx_stream)
        s = jnp.dot(qi_ref[0], pages_bf16(ibuf, slot), preferred_element_type=F32)
        s_ref[:, pl.ds(pl.multiple_of(grp * (g * page), g * page), g * page)] = _sortable_key(head_sum(s))
        return carry

    lax.fori_loop(0, n_groups, score_group, 0)
    s_new = jnp.dot(qi_ref[0], kin_ref[0].astype(BF16), preferred_element_type=F32)
    sc_new = jnp.where((lane < n_new) & (lane <= tok), head_sum(s_new), NEG_INF)
    s_ref[:, past:past + LANES] = _sortable_key(sc_new)
    tail = _sample_tail_chunks(n_pages, page)
    if tail > 1:
        s_ref[:, past + LANES:past + tail * LANES] = _sortable_key(jnp.full((rows, (tail - 1) * LANES), NEG_INF, F32))

    thr = _select_threshold(s_ref, 0, rows, past // LANES + tail, topk, n_valid=n_new, unroll=SAMPLE_UNROLL)[:, :1]

    m_ref[...] = jnp.full(m_ref.shape, NEG_INF, F32)
    l_ref[...] = jnp.zeros(l_ref.shape, F32)
    acc_ref[...] = jnp.zeros(acc_ref.shape, F32)

    def attend(kht, vht, sel):
        bias = jnp.where(sel, 0.0, NEG_INF)
        logits = jnp.concatenate(
            [jnp.dot(q_ref[0, h], kht(h), preferred_element_type=F32) + bias for h in range(N_HEADS)], axis=0)
        m_old = m_ref[...]
        m_new = jnp.maximum(m_old, jnp.max(logits, axis=1, keepdims=True))
        m_safe = jnp.where(m_new == NEG_INF, 0.0, m_new)
        p = jnp.exp(logits - m_safe).astype(BF16)
        alpha = jnp.exp(m_old - m_safe)
        l_ref[...] = l_ref[...] * alpha + jnp.sum(p.astype(F32), axis=1, keepdims=True)
        m_ref[...] = m_new
        pv = jnp.concatenate(
            [lax.dot_general(p[h * rows:(h + 1) * rows], vht(h), nt, preferred_element_type=F32)
             for h in range(N_HEADS)], axis=0)
        acc_ref[...] = acc_ref[...] * alpha + pv

    def attend_group(grp, carry):
        slot = advance(grp, *kv_streams)
        sel = s_ref[:, pl.ds(pl.multiple_of(grp * (g * page), g * page), g * page)] >= thr
        attend(lambda h: pages_bf16(kbuf, slot, h), lambda h: pages_bf16(vbuf, slot, h), sel)
        return carry

    lax.fori_loop(0, n_groups, attend_group, 0)
    sel_new = (s_ref[:, past:past + LANES] >= thr) & (lane < n_new) & (lane <= tok)
    attend(lambda h: kn_ref[0, h].astype(BF16), lambda h: vn_ref[0, h].astype(BF16), sel_new)

    out = acc_ref[...] / l_ref[...]
    for h in range(N_HEADS):
        o_ref[0, h] = out[h * rows:(h + 1) * rows].astype(o_ref.dtype)


def _attn_sample(page_table, qi_ht, wi_ht, q_ht, ki_new, k_new, v_new, cache_idx_k, cache_k, cache_v, n_new):
    nb, n_pages = page_table.shape
    page = cache_idx_k.shape[2]
    past = n_pages * page
    topk = min(TOPK_MAX, (past + n_new) // 4)
    g = PAGE_GROUP
    blk = lambda a: pl.BlockSpec((1,) + a.shape[1:], lambda b, pt: (b,) + (0,) * (a.ndim - 1))
    any_spec = pl.BlockSpec(memory_space=pl.ANY)
    return pl.pallas_call(
        functools.partial(_attn_sample_kernel, n_new=n_new, topk=topk),
        grid_spec=pltpu.PrefetchScalarGridSpec(
            num_scalar_prefetch=1,
            grid=(nb,),
            in_specs=[blk(qi_ht), blk(wi_ht), blk(q_ht), blk(ki_new), blk(k_new), blk(v_new),
                      any_spec, any_spec, any_spec],
            out_specs=pl.BlockSpec((1, N_HEADS, TOK_ROWS, HEAD_DIM), lambda b, pt: (b, 0, 0, 0)),
            scratch_shapes=[pltpu.VMEM((TOK_ROWS, past + _sample_tail_chunks(n_pages, page) * LANES), I32),
                            pltpu.VMEM((PAGE_SLOTS, g, IDX_DIM, page), F32),
                            pltpu.VMEM((PAGE_SLOTS, g, N_HEADS, HEAD_DIM, page), F32),
                            pltpu.VMEM((PAGE_SLOTS, g, N_HEADS, HEAD_DIM, page), F32),
                            pltpu.SemaphoreType.DMA((3, PAGE_SLOTS)),
                            pltpu.VMEM((N_HEADS * TOK_ROWS, HEAD_DIM), F32),
                            pltpu.VMEM((N_HEADS * TOK_ROWS, 1), F32), pltpu.VMEM((N_HEADS * TOK_ROWS, 1), F32)]),
        out_shape=jax.ShapeDtypeStruct((nb, N_HEADS, TOK_ROWS, HEAD_DIM), BF16),
        compiler_params=_cparams("arbitrary"),
        name="attn_sample",
    )(page_table, qi_ht, wi_ht, q_ht, ki_new, k_new, v_new, cache_idx_k, cache_k, cache_v)


CONV_HALO = 32


def _conv_prompt_kernel(a_ref, b_ref, dw_ref, db_ref, y_ref, cat_ref, *, width):
    tm = a_ref.shape[0]
    cat_ref[0:tm, :] = a_ref[...]
    cat_ref[tm:tm + CONV_HALO, :] = b_ref[...]
    lead = CONV_HALO - (width - 1)
    acc = jnp.zeros(y_ref.shape, F32) + db_ref[...]
    for w in range(width):
        acc = acc + dw_ref[w:w + 1, :] * cat_ref[lead + w:lead + w + tm, :]
    y_ref[...] = acc


def _conv_prompt(u, dw, db, tm):
    t, c = u.shape
    width = dw.shape[0]
    u_pad = jnp.concatenate([jnp.zeros((CONV_HALO, c), u.dtype), u], axis=0)
    return pl.pallas_call(
        functools.partial(_conv_prompt_kernel, width=width),
        grid=(t // tm,),
        in_specs=[pl.BlockSpec((tm, c), lambda i: (i, 0)),
                  pl.BlockSpec((CONV_HALO, c), lambda i: ((i + 1) * (tm // CONV_HALO), 0)),
                  _const_spec(dw.shape), _const_spec((1, c))],
        out_specs=pl.BlockSpec((tm, c), lambda i: (i, 0)),
        out_shape=jax.ShapeDtypeStruct((t, c), F32),
        scratch_shapes=[pltpu.VMEM((tm + CONV_HALO, c), F32)],
        compiler_params=_cparams("parallel"),
        name="conv_prompt",
    )(u_pad, u_pad, dw, db.reshape(1, c))


def _conv_sample_kernel(u_ref, dw_ref, db_ref, y_ref, *, width):
    for t in range(y_ref.shape[0]):
        acc = jnp.zeros(y_ref.shape[1:], F32) + db_ref[...]
        for w in range(width):
            acc = acc + dw_ref[w:w + 1, :] * u_ref[t + w]
        y_ref[t] = acc


def _conv_sample(u_tm, dw, db):
    rows, b, c = u_tm.shape
    width = dw.shape[0]
    t = rows - (width - 1)
    return pl.pallas_call(
        functools.partial(_conv_sample_kernel, width=width),
        out_shape=jax.ShapeDtypeStruct((t, b, c), F32),
        name="conv_sample",
    )(u_tm, dw, db.reshape(1, c))


def _merge_kernel(x_ref, attn_ref, y_ref, ga_ref, gc_ref, lng_ref, lnb_ref, woa_ref, wco_ref, wout_ref, o_ref):
    y = y_ref[...]
    mu = jnp.mean(y, axis=-1, keepdims=True)
    yc = y - mu
    var = jnp.mean(yc * yc, axis=-1, keepdims=True)
    z = yc * lax.rsqrt(var + EPS) * lng_ref[...] + lnb_ref[...]
    z = z * jax.nn.sigmoid(z)
    conv_y = jnp.dot(z.astype(BF16), wco_ref[...], preferred_element_type=F32)
    a = jnp.dot(attn_ref[...], woa_ref[...], preferred_element_type=F32)
    m = ga_ref[...] * a + gc_ref[...] * conv_y
    o_ref[...] = x_ref[...] + jnp.dot(m.astype(BF16), wout_ref[...], preferred_element_type=F32)


def _merge(x2d, attn, y, sga, sgc, ln_g, ln_b, woa, wco, wout, tm):
    n, d = x2d.shape
    c = y.shape[1]
    row = lambda w: pl.BlockSpec((tm, w), lambda i: (i, 0))
    return pl.pallas_call(
        _merge_kernel,
        grid=(n // tm,),
        in_specs=[row(d), row(attn.shape[1]), row(c), row(d), row(d), _const_spec((1, c)), _const_spec((1, c)),
                  _const_spec(woa.shape), _const_spec(wco.shape), _const_spec(wout.shape)],
        out_specs=row(d),
        out_shape=jax.ShapeDtypeStruct((n, d), F32),
        compiler_params=_cparams("parallel"),
        name="merge",
    )(x2d, attn, y, sga, sgc, ln_g.reshape(1, c), ln_b.reshape(1, c), woa, wco, wout)


def _extract_topk(s, k):
    rows = s.shape[0]
    ridx = lax.broadcasted_iota(I32, s.shape, 0)
    rank = jnp.full(s.shape, float(k), F32)
    vals = []
    for r in range(k):
        m = jnp.max(s, axis=0, keepdims=True)
        first = jnp.min(jnp.where(s == m, ridx, rows), axis=0, keepdims=True)
        hit = ridx == first
        rank = jnp.where(hit, float(r), rank)
        s = jnp.where(hit, NEG_INF, s)
        vals.append(m)
    return jnp.concatenate(vals, axis=0), rank


def _peer_candidates(k):
    return [(r1, r2) for r1 in range(k) for r2 in range(k) if (r1 + 1) * (r2 + 1) <= k]


BF16_ROWS = 16


def _peer_kernel(x_ref, g2_ref, wqt_ref, keys_ref, u_ref, vt_ref, o_ref,
                 ht_ref, qt_ref, s_ref, rank_ref, vals_ref, thr1_ref, e1_ref, rank2_ref, e2_ref, p_ref, acc_ref):
    e = pl.program_id(1)
    n_e = pl.num_programs(1) - 1
    tn = x_ref.shape[0]
    n_keys = keys_ref.shape[1]
    n_heads = keys_ref.shape[0] // 2
    k = PEER_TOPK
    packed = (n_keys // BF16_ROWS, BF16_ROWS, tn)

    @pl.when(e == 0)
    def _select():
        x = x_ref[...]
        ms = jnp.mean(x * x, axis=-1, keepdims=True)
        h = x * lax.rsqrt(ms + EPS) * g2_ref[...]
        ht_ref[...] = h.T.astype(BF16)
        qt_ref[...] = jnp.dot(wqt_ref[...], ht_ref[...], preferred_element_type=F32).astype(BF16)
        half = keys_ref.shape[2]

        def half_scores(hc, carry):
            q_hc = qt_ref[pl.ds(pl.multiple_of(hc * half, half), half), :]
            s = jnp.dot(keys_ref[hc], q_hc, preferred_element_type=F32)
            s_ref[hc] = s
            vals, rank = _extract_topk(s, k)
            vals_ref[hc] = vals
            rank_ref[hc] = rank
            return carry

        lax.fori_loop(0, 2 * n_heads, half_scores, 0)

        pairs = _peer_candidates(k)
        pad = (-len(pairs)) % 8

        def head_weights(h, carry):
            v1 = vals_ref[2 * h]
            v2 = vals_ref[2 * h + 1]
            cand = jnp.concatenate([v1[r1:r1 + 1] + v2[r2:r2 + 1] for r1, r2 in pairs]
                                   + [jnp.full((pad, tn), NEG_INF, F32)], axis=0)
            _, crank = _extract_topk(cand, k)
            chosen = crank < float(k)
            ex1 = jnp.exp(v1 - v1[0:1])
            ex2 = jnp.exp(v2 - v2[0:1])
            zsum = jnp.zeros((1, tn), F32)
            counts = [jnp.zeros((1, tn), F32) for _ in range(k)]
            for row, (r1, r2) in enumerate(pairs):
                c = chosen[row:row + 1]
                zsum = zsum + jnp.where(c, ex1[r1:r1 + 1] * ex2[r2:r2 + 1], 0.0)
                counts[r1] = counts[r1] + jnp.where(c, 1.0, 0.0)
            rank1 = rank_ref[2 * h]
            thr1 = jnp.zeros((n_keys, tn), F32)
            for r1 in range(k):
                thr1 = jnp.where(rank1 == float(r1), counts[r1], thr1)
            thr1_ref[h] = thr1
            e1_ref[h] = jnp.exp(s_ref[2 * h] - v1[0:1]) / zsum
            rank2_ref[h] = rank_ref[2 * h + 1].reshape(packed).astype(BF16)
            e2_ref[h] = jnp.exp(s_ref[2 * h + 1] - v2[0:1]).reshape(packed).astype(BF16)
            return carry

        lax.fori_loop(0, n_heads, head_weights, 0)
        acc_ref[...] = jnp.zeros(acc_ref.shape, F32)
        p_ref[1] = jnp.zeros(p_ref.shape[1:], BF16)

    te = u_ref.shape[0]
    i0 = jnp.minimum(e, n_e - 1) * (te // n_keys)

    def step(new, old):
        tc = min(tn, PEER_TOKEN_CHUNK)
        packed_c = packed[:2] + (tc,)
        cols = [slice(c * tc, (c + 1) * tc) for c in range(tn // tc)]
        for cs in cols:
            ws = []
            for ii in range(te // n_keys):
                w = jnp.zeros(packed_c, BF16)
                for h in range(n_heads):
                    row = lambda ref: jnp.broadcast_to(ref[h, pl.ds(i0 + ii, 1), cs], (BF16_ROWS, tc)).astype(BF16)[None]
                    w = w + row(e1_ref) * jnp.where(rank2_ref[h, :, :, cs] < row(thr1_ref), e2_ref[h, :, :, cs],
                                                    jnp.zeros((), BF16))
                ws.append(w)
            acc_ref[:, cs] += jnp.dot(vt_ref[...], p_ref[old, :, cs], preferred_element_type=F32)
            act = jax.nn.gelu(jnp.dot(u_ref[...], ht_ref[:, cs], preferred_element_type=F32))
            for ii, w in enumerate(ws):
                g = act[ii * n_keys:(ii + 1) * n_keys].reshape(packed_c).astype(BF16)
                p_ref[new, ii * n_keys:(ii + 1) * n_keys, cs] = (w * g).reshape(n_keys, tc)

    for parity in (0, 1):
        pl.when(e % 2 == parity)(functools.partial(step, parity, 1 - parity))

    @pl.when(e == n_e)
    def _finish():
        o_ref[...] = x_ref[...] + acc_ref[...].T


def _peer(x1, ln2_g, wqt, keys, u_bf, vt_bf, tn, te):
    n, d = x1.shape
    n_exp = u_bf.shape[0]
    hc, n_keys, half = keys.shape
    k = PEER_TOPK
    n_e = n_exp // te
    n_h = hc // 2
    packed = (n_h, n_keys // BF16_ROWS, BF16_ROWS, tn)
    return pl.pallas_call(
        _peer_kernel,
        grid=(n // tn, n_e + 1),
        in_specs=[pl.BlockSpec((tn, d), lambda i, e: (i, 0)), _const_spec((1, d)), _const_spec(wqt.shape),
                  _const_spec(keys.shape), pl.BlockSpec((te, d), lambda i, e: (jnp.minimum(e, n_e - 1), 0)),
                  pl.BlockSpec((d, te), lambda i, e: (0, jnp.maximum(e - 1, 0)))],
        out_specs=pl.BlockSpec((tn, d), lambda i, e: (i, 0)),
        out_shape=jax.ShapeDtypeStruct((n, d), F32),
        scratch_shapes=[pltpu.VMEM((d, tn), BF16), pltpu.VMEM((hc * half, tn), BF16),
                        pltpu.VMEM((hc, n_keys, tn), F32), pltpu.VMEM((hc, n_keys, tn), F32),
                        pltpu.VMEM((hc, k, tn), F32), pltpu.VMEM((n_h, n_keys, tn), F32),
                        pltpu.VMEM((n_h, n_keys, tn), F32), pltpu.VMEM(packed, BF16), pltpu.VMEM(packed, BF16),
                        pltpu.VMEM((2, te, tn), BF16), pltpu.VMEM((d, tn), F32)],
        compiler_params=_cparams("parallel", "arbitrary"),
        name="peer",
    )(x1, ln2_g.reshape(1, d), wqt, keys, u_bf, vt_bf)


def kernel(x_prompt, x_sample, cache_k, cache_v, cache_idx_k, state_conv, page_table, ln1_g, w_in, q_norm_g, k_norm_g, w_o_attn, conv_dw_w, conv_dw_b, conv_ln_g, conv_ln_b, w_conv_out, w_out, ln2_g, peer_w_q, peer_sub_keys, peer_u, peer_v):
    bp, t, d = x_prompt.shape
    db, ds, _ = x_sample.shape
    assert bp == 1 and ds <= TOK_ROWS
    hist = conv_dw_w.shape[0] - 1
    H, dh = N_HEADS, HEAD_DIM

    w_packed = _pack_w_in(w_in, d)
    woa, wco, wout = w_o_attn.astype(BF16), w_conv_out.astype(BF16), w_out.astype(BF16)
    wqt = peer_w_q.reshape(d, -1).T.astype(BF16)
    sub_keys = peer_sub_keys.reshape(-1, peer_sub_keys.shape[2], peer_sub_keys.shape[3]).astype(BF16)
    u_bf = peer_u.astype(BF16)
    vt_bf = peer_v.T.astype(BF16)

    def tail(x1, tn):
        return _peer(x1, ln2_g, wqt, sub_keys, u_bf, vt_bf, tn, PEER_EXPERT_TILE)

    qs, k, kt, v, vb, qib, ki, kit, wi, u, ga, gc = _proj(x_prompt[0], ln1_g, w_packed, q_norm_g, k_norm_g, ROW_TILE)
    logit_bound = 1.02 * (dh ** 0.5) * jnp.max(jnp.abs(q_norm_g)) * jnp.max(jnp.abs(k_norm_g))
    attn = _attn_prompt(qs, qib, wi, kt, vb, kit, logit_bound, Q_TILE)
    y = _conv_prompt(u, conv_dw_w, conv_dw_b, ROW_TILE)
    x1 = _merge(x_prompt[0], attn, y, ga, gc, conv_ln_g, conv_ln_b, woa, wco, wout, ROW_TILE)
    y_prompt = tail(x1, PEER_TOKEN_TILE).reshape(1, t, d)
    conv_p_state = jnp.concatenate([jnp.zeros((hist, u.shape[1]), F32), u], axis=0)[-hist:].reshape(1, hist, -1)

    n_s = db * ds
    qs_s, k_s, _, v_s, _, qib_s, ki_s, _, wi_s, u_s, ga_s, gc_s = _proj(
        x_sample.reshape(n_s, d), ln1_g, w_packed, q_norm_g, k_norm_g, n_s)

    def per_head(a, width):
        a = a.reshape(db, ds, H, width).transpose(0, 2, 1, 3)
        return jnp.pad(a, ((0, 0), (0, 0), (0, TOK_ROWS - ds), (0, 0)))

    qi_ht = per_head(qib_s, LANES)[..., :IDX_DIM].reshape(db, H * TOK_ROWS, IDX_DIM)
    wi_ht = per_head(wi_s, LANES).reshape(db, H * TOK_ROWS, LANES)
    q_ht = per_head(qs_s, LANES)[..., :dh]
    pad_keys = lambda a: jnp.pad(a, ((0, 0),) * (a.ndim - 1) + ((0, LANES - ds),))
    ki_new = pad_keys(ki_s.reshape(db, ds, IDX_DIM).transpose(0, 2, 1))
    k_new = pad_keys(k_s.reshape(db, ds, H, dh).transpose(0, 2, 3, 1))
    v_new = pad_keys(v_s.reshape(db, ds, H, dh).transpose(0, 2, 3, 1))
    attn_s = _attn_sample(page_table, qi_ht, wi_ht, q_ht, ki_new, k_new, v_new, cache_idx_k.transpose(0, 2, 1),
                          cache_k.transpose(0, 2, 3, 1), cache_v.transpose(0, 2, 3, 1), ds)
    attn_s = attn_s[:, :, :ds].transpose(0, 2, 1, 3).reshape(n_s, H * dh)
    u_pad_s = jnp.concatenate([state_conv, u_s.reshape(db, ds, -1)], axis=1)
    y_s = _conv_sample(u_pad_s.transpose(1, 0, 2), conv_dw_w, conv_dw_b).transpose(1, 0, 2).reshape(n_s, -1)
    x1_s = _merge(x_sample.reshape(n_s, d), attn_s, y_s, ga_s, gc_s, conv_ln_g, conv_ln_b, woa, wco, wout, n_s)
    y_sample = tail(x1_s, n_s).reshape(db, ds, d)

    return (y_prompt, y_sample,
            k.reshape(1, t, H, dh), v.reshape(1, t, H, dh), ki.reshape(1, t, IDX_DIM), conv_p_state,
            k_s.reshape(db, ds, H, dh), v_s.reshape(db, ds, H, dh), ki_s.reshape(db, ds, IDX_DIM),
            u_pad_s[:, -hist:])
```

```python
import functools

import jax
import jax.numpy as jnp
import numpy as np
from jax import lax
from jax.experimental import pallas as pl
from jax.experimental.pallas import tpu as pltpu

F32 = jnp.float32
BF16 = jnp.bfloat16
I32 = jnp.int32

EPS = 1e-6
N_HEADS = 8
HEAD_DIM = 64
ATTN_DIM = N_HEADS * HEAD_DIM
IDX_HEADS = 8
IDX_DIM = 64
TOPK_MAX = 256
CONV_DIM = 512
PEER_TOPK = 16
LANES = 128
SELECT_ROWS = 128
ROW_TILE = 256
Q_TILE = 256
KEY_BLOCK = 1024
PEER_TOKEN_TILE = 512
PEER_EXPERT_TILE = 512
PEER_TOKEN_CHUNK = 256
VMEM_LIMIT = 56 * 1024 * 1024

NEG_INF = float("-inf")
INT_MIN = -(2 ** 31)


def _cparams(*sem):
    return pltpu.CompilerParams(dimension_semantics=sem, vmem_limit_bytes=VMEM_LIMIT)


def _const_spec(shape):
    nd = len(shape)
    return pl.BlockSpec(shape, lambda *_: (0,) * nd, pipeline_mode=pl.Buffered(1))


def _group_rms(x, gsum_ref, gain):
    x2 = x * x
    hi = x2.astype(BF16)
    lo = (x2 - hi.astype(F32)).astype(BF16)
    ss = (jnp.dot(hi, gsum_ref[...], preferred_element_type=F32)
          + jnp.dot(lo, gsum_ref[...], preferred_element_type=F32))
    return x * lax.rsqrt(ss * (1.0 / HEAD_DIM) + EPS) * gain


def _pad_heads(x):
    lo_half = lax.broadcasted_iota(I32, (x.shape[0], LANES), 1) < HEAD_DIM
    outs = []
    for p in range(x.shape[1] // LANES):
        slab = x[:, p * LANES:(p + 1) * LANES]
        outs.append(jnp.where(lo_half, slab, 0.0))
        outs.append(jnp.where(lo_half, pltpu.roll(slab, HEAD_DIM, 1), 0.0))
    return jnp.concatenate(outs, axis=1)


def _proj_kernel(x_ref, g1_ref, w_ref, qg_ref, kg_ref, gsum_ref,
                 q_ref, k_ref, kt_ref, v_ref, vb_ref, qi_ref, ki_ref, kit_ref, wi_ref, u_ref, ga_ref, gc_ref):
    x = x_ref[...]
    tm = x.shape[0]
    ms = jnp.mean(x * x, axis=-1, keepdims=True)
    h = (x * lax.rsqrt(ms + EPS) * g1_ref[...]).astype(BF16)

    def seg(lo, width):
        return jnp.dot(h, w_ref[:, lo:lo + width], preferred_element_type=F32)

    A = ATTN_DIM
    q = _group_rms(seg(0, A), gsum_ref, qg_ref[...])
    q_ref[...] = _pad_heads(q * (HEAD_DIM ** -0.5)).astype(BF16)
    k = _group_rms(seg(A, A), gsum_ref, kg_ref[...])
    k_ref[...] = k
    kt_ref[...] = k.T.astype(BF16)
    v = seg(2 * A, A)
    v_ref[...] = v
    vb_ref[...] = v.astype(BF16)
    qi_ref[...] = _pad_heads(seg(3 * A, A)).astype(BF16)
    o = 4 * A
    ki = seg(o, LANES)
    ki_ref[...] = ki[:, :IDX_DIM]
    kit_ref[...] = ki.T[:IDX_DIM].astype(BF16)
    wi = seg(o + LANES, LANES)
    wi_ref[...] = jnp.concatenate(
        [jnp.broadcast_to(wi[:, hh:hh + 1], (tm, LANES)) for hh in range(IDX_HEADS)], axis=1)
    o += 2 * LANES
    a = seg(o, CONV_DIM)
    b = seg(o + CONV_DIM, CONV_DIM)
    u_ref[...] = a * jax.nn.sigmoid(b)
    o += 2 * CONV_DIM
    d = x.shape[-1]
    ga_ref[...] = jax.nn.sigmoid(seg(o, d))
    gc_ref[...] = jax.nn.sigmoid(seg(o + d, d))


def _pack_w_in(w_in, d):
    A = ATTN_DIM
    o = 4 * A
    pad = lambda w: jnp.pad(w, ((0, 0), (0, LANES - w.shape[1])))
    parts = [w_in[:, :o], pad(w_in[:, o:o + IDX_DIM]), pad(w_in[:, o + IDX_DIM:o + IDX_DIM + IDX_HEADS]),
             w_in[:, o + IDX_DIM + IDX_HEADS:]]
    return jnp.concatenate(parts, axis=1).astype(BF16)


def _proj(x2d, ln1_g, w_packed, q_norm_g, k_norm_g, tm):
    n, d = x2d.shape
    A = ATTN_DIM
    wcols = w_packed.shape[1]
    gsum = jnp.asarray(np.kron(np.eye(N_HEADS), np.ones((HEAD_DIM, HEAD_DIM))), BF16)
    row = lambda w: pl.BlockSpec((tm, w), lambda i: (i, 0))
    colT = lambda r: pl.BlockSpec((r, tm), lambda i: (0, i))
    sds = jax.ShapeDtypeStruct
    return pl.pallas_call(
        _proj_kernel,
        grid=(n // tm,),
        in_specs=[row(d), _const_spec((1, d)), _const_spec((d, wcols)), _const_spec((1, A)), _const_spec((1, A)),
                  _const_spec((A, A))],
        out_specs=[row(2 * A), row(A), colT(A), row(A), row(A), row(2 * A), row(IDX_DIM), colT(IDX_DIM),
                   row(IDX_HEADS * LANES), row(CONV_DIM), row(d), row(d)],
        out_shape=[sds((n, 2 * A), BF16), sds((n, A), F32), sds((A, n), BF16), sds((n, A), F32),
                   sds((n, A), BF16), sds((n, 2 * A), BF16), sds((n, IDX_DIM), F32), sds((IDX_DIM, n), BF16),
                   sds((n, IDX_HEADS * LANES), F32), sds((n, CONV_DIM), F32), sds((n, d), F32), sds((n, d), F32)],
        compiler_params=_cparams("parallel"),
        name="proj",
    )(x2d, ln1_g.reshape(1, d), w_packed, jnp.tile(q_norm_g, N_HEADS).reshape(1, A),
      jnp.tile(k_norm_g, N_HEADS).reshape(1, A), gsum)


SMALLEST_NORMAL_KEY = 0x00800000


def _key_to_float(key):
    key = jnp.where((key >= 1) & (key < SMALLEST_NORMAL_KEY), SMALLEST_NORMAL_KEY, key)
    return lax.bitcast_convert_type(key ^ ((key >> 31) & 0x7FFFFFFF), F32)


def _count_rows(s_ref, r0, rows, n_chunks, unroll, pred, partial=False):
    col0 = lax.broadcasted_iota(I32, (rows, LANES), 1)

    def body(c, acc):
        for u in range(unroll):
            off = pl.multiple_of((c * unroll + u) * LANES, LANES)
            acc = acc + jnp.where(pred(s_ref[r0:r0 + rows, pl.ds(off, LANES)], col0 + off), 1, 0)
        return acc

    acc = lax.fori_loop(0, n_chunks // unroll, body, jnp.zeros((rows, LANES), I32))
    return acc if partial else jnp.broadcast_to(jnp.sum(acc, axis=1, keepdims=True), (rows, LANES))


def _lane_total(acc):
    return jnp.broadcast_to(jnp.sum(acc, axis=1, keepdims=True), acc.shape)


def _demote_surplus_ties(s_ref, r0, rows, n_chunks, k, unroll, thr):
    count = functools.partial(_count_rows, s_ref, r0, rows, n_chunks, unroll)
    need = k - count(lambda keys, _: keys > thr)

    def idx_step(it, prefix):
        cand = prefix | jnp.left_shift(jnp.int32(1), 14 - it)
        cnt = count(lambda keys, idx: jnp.where(keys == thr, idx, cand) < cand)
        return jnp.where(cnt < need, cand, prefix)

    last = lax.fori_loop(0, 15, idx_step, jnp.zeros((rows, LANES), I32))
    col0 = lax.broadcasted_iota(I32, (rows, LANES), 1)

    def demote(c, carry):
        off = pl.multiple_of(c * LANES, LANES)
        keys = s_ref[r0:r0 + rows, pl.ds(off, LANES)]
        surplus = jnp.where(keys == thr, col0 + off, 0) > last
        s_ref[r0:r0 + rows, pl.ds(off, LANES)] = jnp.where(surplus, NEG_INF, keys)
        return carry

    lax.fori_loop(0, n_chunks, demote, 0)


def _select_threshold(s_ref, starts, rows, n_chunks, k, n_valid=None, unroll=1):
    def count_ge(keys):
        accs = [_count_rows(s_ref, r0, rows, n_chunks, unroll, lambda x, _, t=_key_to_float(key): x >= t, partial=True)
                for r0, key in zip(starts, keys)]
        return [_lane_total(a) for a in accs]

    zero = jnp.zeros((rows, LANES), I32)
    sign_prefix = tuple(jnp.where(c >= k, zero, zero + INT_MIN) for c in count_ge([zero] * len(starts)))

    row_id = lax.broadcasted_iota(I32, (rows, LANES), 0)
    padding = zero if n_valid is None else jnp.where(row_id < n_valid, 0, 1)

    def bit_step(carry):
        it, prefixes, done = carry
        cands = [p | jnp.left_shift(jnp.int32(1), 30 - it) for p in prefixes]
        cnts = count_ge(cands)
        done = tuple(d | jnp.where(c == k, 1, 0) for d, c in zip(done, cnts))
        return it + 1, tuple(jnp.where(c >= k, cand, p) for c, cand, p in zip(cnts, cands, prefixes)), done

    def unfinished(carry):
        all_done = carry[2][0]
        for d in carry[2][1:]:
            all_done = all_done & d
        return (carry[0] < 31) & (jnp.min(all_done) == 0)

    _, keys, _ = lax.while_loop(unfinished, bit_step,
                                (jnp.int32(0), sign_prefix, tuple(padding for _ in starts)))
    n_ges = count_ge(keys)
    thrs = [_key_to_float(key) for key in keys]
    thrs = [jnp.where(t != t, NEG_INF, t) for t in thrs]
    for r0, thr, n_ge in zip(starts, thrs, n_ges):
        if n_valid is not None:
            n_ge = jnp.where(row_id < n_valid, n_ge, 0)
        pl.when(jnp.max(n_ge) > k)(functools.partial(_demote_surplus_ties, s_ref, r0, rows, n_chunks, k, unroll, thr))
    return thrs


def _softmax_chunk(h, logits, m_ref, l_ref):
    m_old = m_ref[:, h:h + 1]
    m_new = jnp.maximum(m_old, jnp.max(logits, axis=1, keepdims=True))
    m_safe = jnp.where(m_new == NEG_INF, 0.0, m_new)
    p = jnp.exp(logits - m_safe)
    alpha = jnp.exp(m_old - m_safe)
    l_ref[:, h:h + 1] = l_ref[:, h:h + 1] * alpha + jnp.sum(p, axis=1, keepdims=True)
    m_ref[:, h:h + 1] = m_new
    return p, alpha


def _attn_prompt_kernel(q_ref, qi_ref, wi_ref, kit_ref, shift_ref, kt_hbm, vb_hbm, o_ref,
                        s_ref, acc_ref, m_ref, l_ref, kbuf, vbuf, sem, *, topk, online):
    i = pl.program_id(0)
    tq = q_ref.shape[0]
    tk = KEY_BLOCK
    n_blocks = (i * tq + tq + tk - 1) // tk
    rel = (lax.broadcasted_iota(I32, (tq, tk), 1) - lax.broadcasted_iota(I32, (tq, tk), 0)) - i * tq
    lo_half = lax.broadcasted_iota(I32, (tq, LANES), 1) < HEAD_DIM

    def kv_copies(j, slot):
        off = pl.multiple_of(j * tk, tk)
        return (pltpu.make_async_copy(kt_hbm.at[:, pl.ds(off, tk)], kbuf.at[slot], sem.at[0, slot]),
                pltpu.make_async_copy(vb_hbm.at[pl.ds(off, tk), :], vbuf.at[slot], sem.at[1, slot]))

    for c in kv_copies(0, 0):
        c.start()

    def score_block(j, carry):
        off = pl.multiple_of(j * tk, tk)
        kit = kit_ref[:, pl.ds(off, tk)]
        sc = jnp.zeros((tq, tk), F32)
        for h in range(IDX_HEADS):
            s = jnp.dot(qi_ref[:, h * LANES:h * LANES + IDX_DIM], kit, preferred_element_type=F32)
            w = wi_ref[:, h * LANES:(h + 1) * LANES]
            sc = sc + jnp.concatenate([w] * (tk // LANES), axis=1) * jnp.maximum(s, 0.0)
        sc = jnp.where(rel + j * tk <= 0, sc, NEG_INF)
        s_ref[:, pl.ds(off, tk)] = sc
        return carry

    lax.fori_loop(0, n_blocks, score_block, 0)

    n_chunks = n_blocks * (tk // LANES)
    thrs = _select_threshold(s_ref, list(range(0, tq, SELECT_ROWS)), SELECT_ROWS, n_chunks, topk, unroll=tk // LANES)
    thr = jnp.concatenate([t[:, :1] for t in thrs], axis=0)

    if online:
        m_ref[...] = jnp.full(m_ref.shape, NEG_INF, F32)
    l_ref[...] = jnp.zeros(l_ref.shape, F32)
    acc_ref[...] = jnp.zeros(acc_ref.shape, F32)

    def attend_block(j, carry):
        slot = j % 2
        for c in kv_copies(j, slot):
            c.wait()

        @pl.when(j + 1 < n_blocks)
        def _():
            for c in kv_copies(j + 1, 1 - slot):
                c.start()

        off = pl.multiple_of(j * tk, tk)
        keys = s_ref[:, pl.ds(off, tk)]
        chosen = 0.0 if online else -shift_ref[0:1, 0:1]
        bias = jnp.where(rel + j * tk <= 0, jnp.where(keys >= thr, chosen, NEG_INF), NEG_INF)
        for p2 in range(N_HEADS // 2):
            vblk = vbuf[slot, :, p2 * LANES:(p2 + 1) * LANES]
            sl = slice(p2 * LANES, (p2 + 1) * LANES)
            outs, alphas = [], []
            for h in (2 * p2, 2 * p2 + 1):
                kt = kbuf[slot, h * HEAD_DIM:(h + 1) * HEAD_DIM, :]
                logits = jnp.dot(q_ref[:, h * LANES:h * LANES + HEAD_DIM], kt, preferred_element_type=F32) + bias
                if online:
                    p, alpha = _softmax_chunk(h, logits, m_ref, l_ref)
                    alphas.append(alpha)
                else:
                    p = jnp.exp(logits)
                    l_ref[h] += sum(p[:, c * LANES:(c + 1) * LANES] for c in range(tk // LANES))
                outs.append(jnp.dot(p.astype(BF16), vblk, preferred_element_type=F32))
            new = jnp.where(lo_half, outs[0], outs[1])
            if online:
                acc_ref[:, sl] = acc_ref[:, sl] * jnp.where(lo_half, alphas[0], alphas[1]) + new
            else:
                acc_ref[:, sl] += new
        return carry

    lax.fori_loop(0, n_blocks, attend_block, 0)

    for p2 in range(N_HEADS // 2):
        sl = slice(p2 * LANES, (p2 + 1) * LANES)
        if online:
            l0, l1 = l_ref[:, 2 * p2:2 * p2 + 1], l_ref[:, 2 * p2 + 1:2 * p2 + 2]
        else:
            l0, l1 = (jnp.sum(l_ref[h], axis=1, keepdims=True) for h in (2 * p2, 2 * p2 + 1))
        o_ref[:, sl] = (acc_ref[:, sl] * jnp.where(lo_half, 1.0 / l0, 1.0 / l1)).astype(o_ref.dtype)


SAFE_SHIFT = 40.0


def _attn_prompt(q_pad, qi_pad, wi_rep, kt, vb, kit, logit_bound, tq):
    n = q_pad.shape[0]
    assert n % KEY_BLOCK == 0 and KEY_BLOCK % tq == 0
    topk = min(TOPK_MAX, n // 4)
    row = lambda w: pl.BlockSpec((tq, w), lambda i: (i, 0))
    any_spec = pl.BlockSpec(memory_space=pl.ANY)
    shift = jnp.full((1, LANES), logit_bound, F32)

    def call(online):
        stats = (pltpu.VMEM((tq, LANES), F32) if online else pltpu.VMEM((N_HEADS, tq, LANES), F32))
        return pl.pallas_call(
            functools.partial(_attn_prompt_kernel, topk=topk, online=online),
            grid=(n // tq,),
            in_specs=[row(q_pad.shape[1]), row(qi_pad.shape[1]), row(wi_rep.shape[1]),
                      _const_spec(kit.shape), _const_spec(shift.shape), any_spec, any_spec],
            out_specs=row(ATTN_DIM),
            out_shape=jax.ShapeDtypeStruct((n, ATTN_DIM), BF16),
            scratch_shapes=[pltpu.VMEM((tq, n), F32), pltpu.VMEM((tq, ATTN_DIM), F32),
                            pltpu.VMEM((tq, LANES), F32), stats,
                            pltpu.VMEM((2, ATTN_DIM, KEY_BLOCK), BF16), pltpu.VMEM((2, KEY_BLOCK, ATTN_DIM), BF16),
                            pltpu.SemaphoreType.DMA((2, 2))],
            compiler_params=_cparams("arbitrary"),
            name="attn_prompt_online" if online else "attn_prompt",
        )(q_pad, qi_pad, wi_rep, kit, shift, kt, vb)

    return lax.cond(logit_bound <= SAFE_SHIFT, lambda: call(False), lambda: call(True))


PAGE_GROUP = 8
PAGE_SLOTS = 6
TOK_ROWS = 8
SAMPLE_UNROLL = 12


def _sample_tail_chunks(n_pages, page):
    cached = n_pages * page // LANES
    return (-cached) % SAMPLE_UNROLL or SAMPLE_UNROLL


def _attn_sample_kernel(pt_ref, qi_ref, wi_ref, q_ref, kin_ref, kn_ref, vn_ref, cik_hbm, ck_hbm, cv_hbm, o_ref,
                        s_ref, ibuf, kbuf, vbuf, sem, acc_ref, m_ref, l_ref, *, n_new, topk):
    b = pl.program_id(0)
    n_pages = pt_ref.shape[1]
    page = cik_hbm.shape[2]
    g = PAGE_GROUP
    n_groups = n_pages // g
    past = n_pages * page
    rows = TOK_ROWS

    def group_copies(src_hbm, buf, which, grp, slot):
        return [pltpu.make_async_copy(src_hbm.at[pt_ref[b, grp * g + j]], buf.at[slot, j], sem.at[which, slot])
                for j in range(g)]

    def start(src_hbm, buf, which, grp, slot):
        for c in group_copies(src_hbm, buf, which, grp, slot):
            c.start()

    def wait(src_hbm, buf, which, grp, slot):
        for c in group_copies(src_hbm, buf, which, grp, slot):
            c.wait()

    tok = lax.broadcasted_iota(I32, (rows, LANES), 0)
    lane = lax.broadcasted_iota(I32, (rows, LANES), 1)

    def head_sum(s):
        sc = jnp.zeros((rows, s.shape[1]), F32)
        for h in range(IDX_HEADS):
            w = wi_ref[0, h * rows:(h + 1) * rows, :]
            sc = sc + jnp.concatenate([w] * (s.shape[1] // LANES), axis=1) * jnp.maximum(s[h * rows:(h + 1) * rows], 0.0)
        return sc

    nt = (((1,), (1,)), ((), ()))

    def pages_bf16(buf, slot, *lead):
        return jnp.concatenate([buf[(slot, j) + lead] for j in range(g)], axis=1).astype(BF16)

    ahead = PAGE_SLOTS - 1

    def prefetch_first(*streams):
        for grp in range(min(ahead, n_groups)):
            for src_hbm, buf, which in streams:
                start(src_hbm, buf, which, grp, grp)

    def advance(grp, *streams):
        slot = grp % PAGE_SLOTS
        for src_hbm, buf, which in streams:
            wait(src_hbm, buf, which, grp, slot)

        @pl.when(grp + ahead < n_groups)
        def _():
            for src_hbm, buf, which in streams:
                start(src_hbm, buf, which, grp + ahead, (grp + ahead) % PAGE_SLOTS)

        return slot

    idx_stream = (cik_hbm, ibuf, 0)
    kv_streams = ((ck_hbm, kbuf, 1), (cv_hbm, vbuf, 2))
    prefetch_first(idx_stream)
    prefetch_first(*kv_streams)

    def score_group(grp, carry):
        slot = advance(grp, idx_stream)
        s = jnp.dot(qi_ref[0], pages_bf16(ibuf, slot), preferred_element_type=F32)
        s_ref[:, pl.ds(pl.multiple_of(grp * (g * page), g * page), g * page)] = head_sum(s)
        return carry

    lax.fori_loop(0, n_groups, score_group, 0)
    s_new = jnp.dot(qi_ref[0], kin_ref[0].astype(BF16), preferred_element_type=F32)
    sc_new = jnp.where((lane < n_new) & (lane <= tok), head_sum(s_new), NEG_INF)
    s_ref[:, past:past + LANES] = sc_new
    tail = _sample_tail_chunks(n_pages, page)
    if tail > 1:
        s_ref[:, past + LANES:past + tail * LANES] = jnp.full((rows, (tail - 1) * LANES), NEG_INF, F32)

    thr = _select_threshold(s_ref, [0], rows, past // LANES + tail, topk, n_valid=n_new, unroll=SAMPLE_UNROLL)[0][:, :1]

    m_ref[...] = jnp.full(m_ref.shape, NEG_INF, F32)
    l_ref[...] = jnp.zeros(l_ref.shape, F32)
    acc_ref[...] = jnp.zeros(acc_ref.shape, F32)

    def attend(kht, vht, sel):
        bias = jnp.where(sel, 0.0, NEG_INF)
        logits = jnp.concatenate(
            [jnp.dot(q_ref[0, h], kht(h), preferred_element_type=F32) + bias for h in range(N_HEADS)], axis=0)
        m_old = m_ref[...]
        m_new = jnp.maximum(m_old, jnp.max(logits, axis=1, keepdims=True))
        m_safe = jnp.where(m_new == NEG_INF, 0.0, m_new)
        p = jnp.exp(logits - m_safe).astype(BF16)
        alpha = jnp.exp(m_old - m_safe)
        l_ref[...] = l_ref[...] * alpha + jnp.sum(p.astype(F32), axis=1, keepdims=True)
        m_ref[...] = m_new
        pv = jnp.concatenate(
            [lax.dot_general(p[h * rows:(h + 1) * rows], vht(h), nt, preferred_element_type=F32)
             for h in range(N_HEADS)], axis=0)
        acc_ref[...] = acc_ref[...] * alpha + pv

    def attend_group(grp, carry):
        slot = advance(grp, *kv_streams)
        sel = s_ref[:, pl.ds(pl.multiple_of(grp * (g * page), g * page), g * page)] >= thr
        attend(lambda h: pages_bf16(kbuf, slot, h), lambda h: pages_bf16(vbuf, slot, h), sel)
        return carry

    lax.fori_loop(0, n_groups, attend_group, 0)
    sel_new = (s_ref[:, past:past + LANES] >= thr) & (lane < n_new) & (lane <= tok)
    attend(lambda h: kn_ref[0, h].astype(BF16), lambda h: vn_ref[0, h].astype(BF16), sel_new)

    out = acc_ref[...] / l_ref[...]
    for h in range(N_HEADS):
        o_ref[0, h] = out[h * rows:(h + 1) * rows].astype(o_ref.dtype)


def _attn_sample(page_table, qi_ht, wi_ht, q_ht, ki_new, k_new, v_new, cache_idx_k, cache_k, cache_v, n_new):
    nb, n_pages = page_table.shape
    page = cache_idx_k.shape[2]
    past = n_pages * page
    topk = min(TOPK_MAX, (past + n_new) // 4)
    g = PAGE_GROUP
    blk = lambda a: pl.BlockSpec((1,) + a.shape[1:], lambda b, pt: (b,) + (0,) * (a.ndim - 1))
    any_spec = pl.BlockSpec(memory_space=pl.ANY)
    return pl.pallas_call(
        functools.partial(_attn_sample_kernel, n_new=n_new, topk=topk),
        grid_spec=pltpu.PrefetchScalarGridSpec(
            num_scalar_prefetch=1,
            grid=(nb,),
            in_specs=[blk(qi_ht), blk(wi_ht), blk(q_ht), blk(ki_new), blk(k_new), blk(v_new),
                      any_spec, any_spec, any_spec],
            out_specs=pl.BlockSpec((1, N_HEADS, TOK_ROWS, HEAD_DIM), lambda b, pt: (b, 0, 0, 0)),
            scratch_shapes=[pltpu.VMEM((TOK_ROWS, past + _sample_tail_chunks(n_pages, page) * LANES), F32),
                            pltpu.VMEM((PAGE_SLOTS, g, IDX_DIM, page), F32),
                            pltpu.VMEM((PAGE_SLOTS, g, N_HEADS, HEAD_DIM, page), F32),
                            pltpu.VMEM((PAGE_SLOTS, g, N_HEADS, HEAD_DIM, page), F32),
                            pltpu.SemaphoreType.DMA((3, PAGE_SLOTS)),
                            pltpu.VMEM((N_HEADS * TOK_ROWS, HEAD_DIM), F32),
                            pltpu.VMEM((N_HEADS * TOK_ROWS, 1), F32), pltpu.VMEM((N_HEADS * TOK_ROWS, 1), F32)]),
        out_shape=jax.ShapeDtypeStruct((nb, N_HEADS, TOK_ROWS, HEAD_DIM), BF16),
        compiler_params=_cparams("arbitrary"),
        name="attn_sample",
    )(page_table, qi_ht, wi_ht, q_ht, ki_new, k_new, v_new, cache_idx_k, cache_k, cache_v)


CONV_HALO = 32


def _conv_prompt_kernel(a_ref, b_ref, dw_ref, db_ref, y_ref, cat_ref, *, width):
    tm = a_ref.shape[0]
    cat_ref[0:tm, :] = a_ref[...]
    cat_ref[tm:tm + CONV_HALO, :] = b_ref[...]
    lead = CONV_HALO - (width - 1)
    acc = jnp.zeros(y_ref.shape, F32) + db_ref[...]
    for w in range(width):
        acc = acc + dw_ref[w:w + 1, :] * cat_ref[lead + w:lead + w + tm, :]
    y_ref[...] = acc


def _conv_prompt(u, dw, db, tm):
    t, c = u.shape
    width = dw.shape[0]
    u_pad = jnp.concatenate([jnp.zeros((CONV_HALO, c), u.dtype), u], axis=0)
    return pl.pallas_call(
        functools.partial(_conv_prompt_kernel, width=width),
        grid=(t // tm,),
        in_specs=[pl.BlockSpec((tm, c), lambda i: (i, 0)),
                  pl.BlockSpec((CONV_HALO, c), lambda i: ((i + 1) * (tm // CONV_HALO), 0)),
                  _const_spec(dw.shape), _const_spec((1, c))],
        out_specs=pl.BlockSpec((tm, c), lambda i: (i, 0)),
        out_shape=jax.ShapeDtypeStruct((t, c), F32),
        scratch_shapes=[pltpu.VMEM((tm + CONV_HALO, c), F32)],
        compiler_params=_cparams("parallel"),
        name="conv_prompt",
    )(u_pad, u_pad, dw, db.reshape(1, c))


def _conv_sample_kernel(u_ref, dw_ref, db_ref, y_ref, *, width):
    for t in range(y_ref.shape[0]):
        acc = jnp.zeros(y_ref.shape[1:], F32) + db_ref[...]
        for w in range(width):
            acc = acc + dw_ref[w:w + 1, :] * u_ref[t + w]
        y_ref[t] = acc


def _conv_sample(u_tm, dw, db):
    rows, b, c = u_tm.shape
    width = dw.shape[0]
    t = rows - (width - 1)
    return pl.pallas_call(
        functools.partial(_conv_sample_kernel, width=width),
        out_shape=jax.ShapeDtypeStruct((t, b, c), F32),
        name="conv_sample",
    )(u_tm, dw, db.reshape(1, c))


def _merge_kernel(x_ref, attn_ref, y_ref, ga_ref, gc_ref, lng_ref, lnb_ref, woa_ref, wco_ref, wout_ref, o_ref):
    y = y_ref[...]
    mu = jnp.mean(y, axis=-1, keepdims=True)
    yc = y - mu
    var = jnp.mean(yc * yc, axis=-1, keepdims=True)
    z = yc * lax.rsqrt(var + EPS) * lng_ref[...] + lnb_ref[...]
    z = z * jax.nn.sigmoid(z)
    conv_y = jnp.dot(z.astype(BF16), wco_ref[...], preferred_element_type=F32)
    a = jnp.dot(attn_ref[...], woa_ref[...], preferred_element_type=F32)
    m = ga_ref[...] * a + gc_ref[...] * conv_y
    o_ref[...] = x_ref[...] + jnp.dot(m.astype(BF16), wout_ref[...], preferred_element_type=F32)


def _merge(x2d, attn, y, sga, sgc, ln_g, ln_b, woa, wco, wout, tm):
    n, d = x2d.shape
    c = y.shape[1]
    row = lambda w: pl.BlockSpec((tm, w), lambda i: (i, 0))
    return pl.pallas_call(
        _merge_kernel,
        grid=(n // tm,),
        in_specs=[row(d), row(attn.shape[1]), row(c), row(d), row(d), _const_spec((1, c)), _const_spec((1, c)),
                  _const_spec(woa.shape), _const_spec(wco.shape), _const_spec(wout.shape)],
        out_specs=row(d),
        out_shape=jax.ShapeDtypeStruct((n, d), F32),
        compiler_params=_cparams("parallel"),
        name="merge",
    )(x2d, attn, y, sga, sgc, ln_g.reshape(1, c), ln_b.reshape(1, c), woa, wco, wout)


def _extract_topk(s, k, break_ties):
    rows = s.shape[0]
    ridx = lax.broadcasted_iota(I32, s.shape, 0)
    rank = jnp.full(s.shape, float(k), F32)
    vals = []
    for r in range(k):
        m = jnp.max(s, axis=0, keepdims=True)
        hit = s == m
        if break_ties:
            hit = ridx == jnp.min(jnp.where(hit, ridx, rows), axis=0, keepdims=True)
        rank = jnp.where(hit, float(r), rank)
        s = jnp.where(hit, NEG_INF, s)
        vals.append(m)
    return jnp.concatenate(vals, axis=0), rank


def _extract_topk_checked(s_in, k, vals_out, rank_out):
    def run(break_ties):
        vals, rank = _extract_topk(s_in(), k, break_ties)
        if vals_out is not None:
            vals_out[...] = vals
        rank_out[...] = rank
        return rank

    rank = run(False)
    n_hit = jnp.sum(jnp.where(rank < float(k), 1.0, 0.0), axis=0, keepdims=True)

    @pl.when(jnp.max(jnp.abs(n_hit - float(k))) > 0.0)
    def _redo_with_tie_breaking():
        run(True)


def _peer_candidates(k):
    return [(r1, r2) for r1 in range(k) for r2 in range(k) if (r1 + 1) * (r2 + 1) <= k]


BF16_ROWS = 16
WD = BF16


def _peer_kernel(x_ref, g2_ref, wqt_ref, keys_ref, u_ref, vt_ref, o_ref,
                 ht_ref, qt_ref, s_ref, rank_ref, vals_ref, cand_ref, crank_ref, thr1_ref, e1_ref, rank2_ref, e2_ref,
                 p_ref, acc_ref):
    e = pl.program_id(1)
    n_e = pl.num_programs(1) - 1
    tn = x_ref.shape[0]
    n_keys = keys_ref.shape[1]
    n_heads = keys_ref.shape[0] // 2
    k = PEER_TOPK
    packed = (n_keys // BF16_ROWS, BF16_ROWS, tn)

    @pl.when(e == 0)
    def _select():
        x = x_ref[...]
        ms = jnp.mean(x * x, axis=-1, keepdims=True)
        h = x * lax.rsqrt(ms + EPS) * g2_ref[...]
        ht_ref[...] = h.T.astype(BF16)
        qt_ref[...] = jnp.dot(wqt_ref[...], ht_ref[...], preferred_element_type=F32).astype(BF16)
        half = keys_ref.shape[2]

        def half_scores(hc, carry):
            q_hc = qt_ref[pl.ds(pl.multiple_of(hc * half, half), half), :]
            s_ref[hc] = jnp.dot(keys_ref[hc], q_hc, preferred_element_type=F32)
            _extract_topk_checked(lambda: s_ref[hc], k, vals_ref.at[hc], rank_ref.at[hc])
            return carry

        lax.fori_loop(0, 2 * n_heads, half_scores, 0)

        pairs = _peer_candidates(k)
        pad = (-len(pairs)) % 8

        def head_weights(h, carry):
            v1 = vals_ref[2 * h]
            v2 = vals_ref[2 * h + 1]
            cand_ref[...] = jnp.concatenate([v1[r1:r1 + 1] + v2[r2:r2 + 1] for r1, r2 in pairs]
                                            + [jnp.full((pad, tn), NEG_INF, F32)], axis=0)
            _extract_topk_checked(lambda: cand_ref[...], k, None, crank_ref)
            chosen = crank_ref[...] < float(k)
            ex1 = jnp.exp(v1 - v1[0:1])
            ex2 = jnp.exp(v2 - v2[0:1])
            zsum = jnp.zeros((1, tn), F32)
            counts = [jnp.zeros((1, tn), F32) for _ in range(k)]
            for row, (r1, r2) in enumerate(pairs):
                c = chosen[row:row + 1]
                zsum = zsum + jnp.where(c, ex1[r1:r1 + 1] * ex2[r2:r2 + 1], 0.0)
                counts[r1] = counts[r1] + jnp.where(c, 1.0, 0.0)
            rank1 = rank_ref[2 * h]
            thr1 = jnp.zeros((n_keys, tn), F32)
            for r1 in range(k):
                thr1 = jnp.where(rank1 == float(r1), counts[r1], thr1)
            thr1_ref[h] = thr1
            e1_ref[h] = jnp.exp(s_ref[2 * h] - v1[0:1]) / zsum
            rank2_ref[h] = rank_ref[2 * h + 1].reshape(packed).astype(WD)
            e2_ref[h] = jnp.exp(s_ref[2 * h + 1] - v2[0:1]).reshape(packed).astype(WD)
            return carry

        lax.fori_loop(0, n_heads, head_weights, 0)
        acc_ref[...] = jnp.zeros(acc_ref.shape, F32)
        p_ref[1] = jnp.zeros(p_ref.shape[1:], BF16)

    te = u_ref.shape[0]
    i0 = jnp.minimum(e, n_e - 1) * (te // n_keys)

    def step(new, old):
        tc = min(tn, PEER_TOKEN_CHUNK)
        packed_c = packed[:2] + (tc,)
        cols = [slice(c * tc, (c + 1) * tc) for c in range(tn // tc)]
        for cs in cols:
            ws = []
            for ii in range(te // n_keys):
                w = jnp.zeros(packed_c, WD)
                for h in range(n_heads):
                    row = lambda ref: jnp.broadcast_to(ref[h, pl.ds(i0 + ii, 1), cs], (BF16_ROWS, tc)).astype(WD)[None]
                    w = w + row(e1_ref) * jnp.where(rank2_ref[h, :, :, cs] < row(thr1_ref), e2_ref[h, :, :, cs],
                                                    jnp.zeros((), WD))
                ws.append(w)
            acc_ref[:, cs] += jnp.dot(vt_ref[...], p_ref[old, :, cs], preferred_element_type=F32)
            act = jax.nn.gelu(jnp.dot(u_ref[...], ht_ref[:, cs], preferred_element_type=F32))
            for ii, w in enumerate(ws):
                g = act[ii * n_keys:(ii + 1) * n_keys].reshape(packed_c).astype(WD)
                p_ref[new, ii * n_keys:(ii + 1) * n_keys, cs] = (w * g).reshape(n_keys, tc).astype(BF16)

    for parity in (0, 1):
        pl.when(e % 2 == parity)(functools.partial(step, parity, 1 - parity))

    @pl.when(e == n_e)
    def _finish():
        o_ref[...] = x_ref[...] + acc_ref[...].T


def _peer(x1, ln2_g, wqt, keys, u_bf, vt_bf, tn, te):
    n, d = x1.shape
    n_exp = u_bf.shape[0]
    hc, n_keys, half = keys.shape
    k = PEER_TOPK
    n_e = n_exp // te
    n_h = hc // 2
    n_cand = -(-len(_peer_candidates(k)) // 8) * 8
    packed = (n_h, n_keys // BF16_ROWS, BF16_ROWS, tn)
    return pl.pallas_call(
        _peer_kernel,
        grid=(n // tn, n_e + 1),
        in_specs=[pl.BlockSpec((tn, d), lambda i, e: (i, 0)), _const_spec((1, d)), _const_spec(wqt.shape),
                  _const_spec(keys.shape), pl.BlockSpec((te, d), lambda i, e: (jnp.minimum(e, n_e - 1), 0)),
                  pl.BlockSpec((d, te), lambda i, e: (0, jnp.maximum(e - 1, 0)))],
        out_specs=pl.BlockSpec((tn, d), lambda i, e: (i, 0)),
        out_shape=jax.ShapeDtypeStruct((n, d), F32),
        scratch_shapes=[pltpu.VMEM((d, tn), BF16), pltpu.VMEM((hc * half, tn), BF16),
                        pltpu.VMEM((hc, n_keys, tn), F32), pltpu.VMEM((hc, n_keys, tn), F32),
                        pltpu.VMEM((hc, k, tn), F32), pltpu.VMEM((n_cand, tn), F32), pltpu.VMEM((n_cand, tn), F32),
                        pltpu.VMEM((n_h, n_keys, tn), F32),
                        pltpu.VMEM((n_h, n_keys, tn), F32), pltpu.VMEM(packed, WD), pltpu.VMEM(packed, WD),
                        pltpu.VMEM((2, te, tn), BF16), pltpu.VMEM((d, tn), F32)],
        compiler_params=_cparams("parallel", "arbitrary"),
        name="peer",
    )(x1, ln2_g.reshape(1, d), wqt, keys, u_bf, vt_bf)


def kernel(x_prompt, x_sample, cache_k, cache_v, cache_idx_k, state_conv, page_table, ln1_g, w_in, q_norm_g, k_norm_g, w_o_attn, conv_dw_w, conv_dw_b, conv_ln_g, conv_ln_b, w_conv_out, w_out, ln2_g, peer_w_q, peer_sub_keys, peer_u, peer_v):
    bp, t, d = x_prompt.shape
    db, ds, _ = x_sample.shape
    assert bp == 1 and ds <= TOK_ROWS
    hist = conv_dw_w.shape[0] - 1
    H, dh = N_HEADS, HEAD_DIM

    w_packed = _pack_w_in(w_in, d)
    woa, wco, wout = w_o_attn.astype(BF16), w_conv_out.astype(BF16), w_out.astype(BF16)
    wqt = peer_w_q.reshape(d, -1).T.astype(BF16)
    sub_keys = peer_sub_keys.reshape(-1, peer_sub_keys.shape[2], peer_sub_keys.shape[3]).astype(BF16)
    u_bf = peer_u.astype(BF16)
    vt_bf = peer_v.T.astype(BF16)

    def tail(x1, tn):
        return _peer(x1, ln2_g, wqt, sub_keys, u_bf, vt_bf, tn, PEER_EXPERT_TILE)

    qs, k, kt, v, vb, qib, ki, kit, wi, u, ga, gc = _proj(x_prompt[0], ln1_g, w_packed, q_norm_g, k_norm_g, ROW_TILE)
    logit_bound = 1.02 * (dh ** 0.5) * jnp.max(jnp.abs(q_norm_g)) * jnp.max(jnp.abs(k_norm_g))
    attn = _attn_prompt(qs, qib, wi, kt, vb, kit, logit_bound, Q_TILE)
    y = _conv_prompt(u, conv_dw_w, conv_dw_b, ROW_TILE)
    x1 = _merge(x_prompt[0], attn, y, ga, gc, conv_ln_g, conv_ln_b, woa, wco, wout, ROW_TILE)
    y_prompt = tail(x1, PEER_TOKEN_TILE).reshape(1, t, d)
    conv_p_state = jnp.concatenate([jnp.zeros((hist, u.shape[1]), F32), u], axis=0)[-hist:].reshape(1, hist, -1)

    n_s = db * ds
    qs_s, k_s, _, v_s, _, qib_s, ki_s, _, wi_s, u_s, ga_s, gc_s = _proj(
        x_sample.reshape(n_s, d), ln1_g, w_packed, q_norm_g, k_norm_g, n_s)

    def per_head(a, width):
        a = a.reshape(db, ds, H, width).transpose(0, 2, 1, 3)
        return jnp.pad(a, ((0, 0), (0, 0), (0, TOK_ROWS - ds), (0, 0)))

    qi_ht = per_head(qib_s, LANES)[..., :IDX_DIM].reshape(db, H * TOK_ROWS, IDX_DIM)
    wi_ht = per_head(wi_s, LANES).reshape(db, H * TOK_ROWS, LANES)
    q_ht = per_head(qs_s, LANES)[..., :dh]
    pad_keys = lambda a: jnp.pad(a, ((0, 0),) * (a.ndim - 1) + ((0, LANES - ds),))
    ki_new = pad_keys(ki_s.reshape(db, ds, IDX_DIM).transpose(0, 2, 1))
    k_new = pad_keys(k_s.reshape(db, ds, H, dh).transpose(0, 2, 3, 1))
    v_new = pad_keys(v_s.reshape(db, ds, H, dh).transpose(0, 2, 3, 1))
    attn_s = _attn_sample(page_table, qi_ht, wi_ht, q_ht, ki_new, k_new, v_new, cache_idx_k.transpose(0, 2, 1),
                          cache_k.transpose(0, 2, 3, 1), cache_v.transpose(0, 2, 3, 1), ds)
    attn_s = attn_s[:, :, :ds].transpose(0, 2, 1, 3).reshape(n_s, H * dh)
    u_pad_s = jnp.concatenate([state_conv, u_s.reshape(db, ds, -1)], axis=1)
    y_s = _conv_sample(u_pad_s.transpose(1, 0, 2), conv_dw_w, conv_dw_b).transpose(1, 0, 2).reshape(n_s, -1)
    x1_s = _merge(x_sample.reshape(n_s, d), attn_s, y_s, ga_s, gc_s, conv_ln_g, conv_ln_b, woa, wco, wout, n_s)
    y_sample = tail(x1_s, n_s).reshape(db, ds, d)

    return (y_prompt, y_sample,
            k.reshape(1, t, H, dh), v.reshape(1, t, H, dh), ki.reshape(1, t, IDX_DIM), conv_p_state,
            k_s.reshape(db, ds, H, dh), v_s.reshape(db, ds, H, dh), ki_s.reshape(db, ds, IDX_DIM),
            u_pad_s[:, -hist:])
```
